```python
import math
import jax, jax.numpy as jnp
from jax import lax
import numpy as np

D_MODEL = 1024
BATCH = 4
SEQ = 4096
DEPTH = 4
DEC_BATCH = 128
DEC_SEQ = 8
PAST_LEN = 2048
PAGE_SIZE = 128

D_MIX = D_MODEL
CONV_CH = D_MIX // 4
CONV_WIDTH = 31
DIFF_WIDTH = D_MIX // 2
DIFF_HEADS = 4
DIFF_DH = DIFF_WIDTH // (2 * DIFF_HEADS)
GLA_WIDTH = D_MIX - CONV_CH - DIFF_WIDTH
GLA_HEADS = 4
GLA_DH = GLA_WIDTH // GLA_HEADS
GLA_RANK = 16
GLA_TAU = 16.0
GLA_CHUNK = 16
D_FF = ((8 * D_MODEL // 3 + 255) // 256) * 256
Q_BLOCK = 128
EPS = 1e-6

OFF_DQ = 2 * CONV_CH
OFF_DK = OFF_DQ + DIFF_WIDTH
OFF_DV = OFF_DK + DIFF_WIDTH
OFF_GQ = OFF_DV + DIFF_WIDTH
OFF_GK = OFF_GQ + GLA_WIDTH
OFF_GV = OFF_GK + GLA_WIDTH
OFF_GG = OFF_GV + GLA_WIDTH
OFF_GA = OFF_GG + GLA_WIDTH
N_IN = OFF_GA + GLA_RANK

kernel_name = 'hybrid_conv_diffattn_gla_step'


def rms_norm(x, g):
    x32 = x.astype(jnp.float32)
    y = x32 * lax.rsqrt(jnp.mean(x32 * x32, axis=-1, keepdims=True) + EPS)
    return (y * g.astype(jnp.float32)).astype(x.dtype)


def layer_norm(x, g, b):
    x32 = x.astype(jnp.float32)
    xc = x32 - jnp.mean(x32, axis=-1, keepdims=True)
    y = xc * lax.rsqrt(jnp.mean(xc * xc, axis=-1, keepdims=True) + EPS)
    return (y * g.astype(jnp.float32) + b.astype(jnp.float32)).astype(x.dtype)


def swiglu(x, w_in, w_out):
    gate, up = jnp.split(x @ w_in, 2, axis=-1)
    return (jax.nn.silu(gate) * up) @ w_out


def causal_depthwise_conv(u, buf, w, b):
    upad = jnp.concatenate([buf.astype(u.dtype), u], axis=1)
    y = lax.conv_general_dilated(upad, w[:, None, :].astype(u.dtype), window_strides=(1,),
                                 padding='VALID', dimension_numbers=('NWC', 'WIO', 'NWC'),
                                 feature_group_count=u.shape[-1])
    return y + b.astype(u.dtype), upad[:, upad.shape[1] - (CONV_WIDTH - 1):]


def diff_attn_block(q, k, v, q_pos, k_pos, lam):
    s = jnp.einsum('bqhcd,bkhcd->bhcqk', q, k) * (DIFF_DH ** -0.5)
    mask = k_pos[None, :] <= q_pos[:, None]
    p = jax.nn.softmax(jnp.where(mask, s, -jnp.inf), axis=-1)
    a = p[:, :, 0] - lam * p[:, :, 1]
    return jnp.einsum('bhqk,bkhe->bqhe', a, v)


def diff_attention(q, k, v, q_pos, k_pos, lam):
    b, nq = q.shape[:2]
    if nq > Q_BLOCK and nq % Q_BLOCK == 0:
        nb = nq // Q_BLOCK
        qb = jnp.moveaxis(q.reshape(b, nb, Q_BLOCK, *q.shape[2:]), 1, 0)
        pb = q_pos.reshape(nb, Q_BLOCK)
        ob = lax.map(lambda qp: diff_attn_block(qp[0], k, v, qp[1], k_pos, lam), (qb, pb))
        return jnp.moveaxis(ob, 0, 1).reshape(b, nq, *ob.shape[3:])
    return diff_attn_block(q, k, v, q_pos, k_pos, lam)


def gla_recurrence(q, k, v, log_a, s0):
    b, t, h, dk = q.shape
    c = math.gcd(GLA_CHUNK, t)
    n = t // c

    def to_chunks(z):
        return jnp.moveaxis(z.reshape(b, n, c, h, z.shape[-1]), (1, 3), (0, 2))

    mask = jnp.tril(jnp.ones((c, c), dtype=bool))

    def step(s, inp):
        qc, kc, vc, lac = inp
        cum = jnp.cumsum(lac, axis=-2)
        qt = qc * jnp.exp(cum)
        kt = kc * jnp.exp(-cum)
        att = jnp.where(mask, jnp.einsum('bhcd,bhed->bhce', qt, kt), 0.0)
        o = jnp.einsum('bhcd,bhde->bhce', qt, s) + jnp.einsum('bhce,bhef->bhcf', att, vc)
        last = cum[:, :, -1:, :]
        s = jnp.exp(last[:, :, 0, :])[..., None] * s + jnp.einsum('bhcd,bhce->bhde', kc * jnp.exp(last - cum), vc)
        return s, o

    s_fin, o = lax.scan(step, s0.astype(jnp.float32), (to_chunks(q), to_chunks(k), to_chunks(v), to_chunks(log_a)))
    o = jnp.moveaxis(o, (0, 2), (1, 3)).reshape(b, t, h, v.shape[-1])
    return o, s_fin


def token_mixer(h, p, conv_buf, gla_s0, kv_past, layer_idx):
    bsz, t, _ = h.shape
    f32 = jnp.float32
    proj = h @ p['w_in']

    u = proj[..., :CONV_CH] * jax.nn.sigmoid(proj[..., CONV_CH:OFF_DQ])
    cv, new_conv = causal_depthwise_conv(u, conv_buf, p['conv_w'], p['conv_b'])
    conv_out = jax.nn.silu(layer_norm(cv, p['conv_ln_g'], p['conv_ln_b'])) @ p['conv_pw']

    dq = proj[..., OFF_DQ:OFF_DK].reshape(bsz, t, DIFF_HEADS, 2, DIFF_DH)
    k_new = proj[..., OFF_DK:OFF_DV].reshape(bsz, t, DIFF_HEADS, 2 * DIFF_DH)
    v_new = proj[..., OFF_DV:OFF_GQ].reshape(bsz, t, DIFF_HEADS, 2 * DIFF_DH)
    if kv_past is None:
        k_all, v_all, offset = k_new, v_new, 0
    else:
        k_all = jnp.concatenate([kv_past[0].astype(k_new.dtype), k_new], axis=1)
        v_all = jnp.concatenate([kv_past[1].astype(v_new.dtype), v_new], axis=1)
        offset = kv_past[0].shape[1]
    q_pos = offset + jnp.arange(t)
    k_pos = jnp.arange(k_all.shape[1])
    lam_init = 0.8 - 0.6 * math.exp(-0.3 * layer_idx)
    lq1, lk1, lq2, lk2 = p['diff_lam'].astype(f32)
    lam = jnp.exp(jnp.sum(lq1 * lk1)) - jnp.exp(jnp.sum(lq2 * lk2)) + lam_init
    do = diff_attention(dq.astype(f32),
                        k_all.astype(f32).reshape(bsz, -1, DIFF_HEADS, 2, DIFF_DH),
                        v_all.astype(f32), q_pos, k_pos, lam)
    do = rms_norm(do, p['diff_subln_g']) * (1.0 - lam_init)
    diff_out = do.reshape(bsz, t, DIFF_WIDTH).astype(h.dtype)

    def heads(z):
        return z.reshape(bsz, t, GLA_HEADS, GLA_DH).astype(f32)
    gq = heads(proj[..., OFF_GQ:OFF_GK]) * (GLA_DH ** -0.5)
    gk = heads(proj[..., OFF_GK:OFF_GV])
    gv = heads(proj[..., OFF_GV:OFF_GG])
    gg = heads(proj[..., OFF_GG:OFF_GA])
    a_logit = proj[..., OFF_GA:N_IN] @ p['gla_w_gate'] + p['gla_b_gate']
    log_a = heads(jax.nn.log_sigmoid(a_logit.astype(f32))) / GLA_TAU
    go, new_s = gla_recurrence(gq, gk, gv, log_a, gla_s0)
    go = rms_norm(go, p['gla_norm_g']) * jax.nn.silu(gg)
    gla_out = go.reshape(bsz, t, GLA_WIDTH).astype(h.dtype)

    out = jnp.concatenate([conv_out, diff_out, gla_out], axis=-1) @ p['w_out']
    return out, k_new, v_new, new_conv, new_s


def decoder_layer(x, p, conv_buf, gla_s0, kv_past, layer_idx):
    g = p['norm_gain']
    x = x + 0.5 * rms_norm(swiglu(rms_norm(x, g[0]), p['ffn_w_in'][0], p['ffn_w_out'][0]), g[1])
    m, k_new, v_new, new_conv, new_s = token_mixer(rms_norm(x, g[2]), p, conv_buf, gla_s0, kv_past, layer_idx)
    x = x + rms_norm(m, g[3])
    x = x + 0.5 * rms_norm(swiglu(rms_norm(x, g[4]), p['ffn_w_in'][1], p['ffn_w_out'][1]), g[5])
    return x, k_new, v_new, new_conv, new_s


def setup_inputs(seed: int = 0) -> dict:
    key = jax.random.key(seed)
    ks = jax.random.split(key, 22)
    n_pages = PAST_LEN // PAGE_SIZE
    n_used = DEC_BATCH * n_pages
    n_pool = n_used + n_used // 4

    def nrm(k, shape, s):
        return jax.random.normal(k, shape, jnp.float32) * s

    page_table = jax.random.permutation(ks[4], n_pool)[:n_used].reshape(DEC_BATCH, n_pages).astype(jnp.int32)
    return {
        'x_prompt': nrm(ks[0], (BATCH, SEQ, D_MODEL), 1.0),
        'x_sample': nrm(ks[1], (DEC_BATCH, DEC_SEQ, D_MODEL), 1.0),
        'cache_k': nrm(ks[2], (DEPTH, n_pool, PAGE_SIZE, DIFF_HEADS, 2 * DIFF_DH), 1.0),
        'cache_v': nrm(ks[3], (DEPTH, n_pool, PAGE_SIZE, DIFF_HEADS, 2 * DIFF_DH), 1.0),
        'page_table': page_table,
        'state_conv': nrm(ks[5], (DEPTH, DEC_BATCH, CONV_WIDTH - 1, CONV_CH), 0.5),
        'state_gla': nrm(ks[6], (DEPTH, DEC_BATCH, GLA_HEADS, GLA_DH, GLA_DH), 0.5),
        'norm_gain': 1.0 + nrm(ks[7], (DEPTH, 6, D_MODEL), 0.02),
        'ffn_w_in': nrm(ks[8], (DEPTH, 2, D_MODEL, 2 * D_FF), D_MODEL ** -0.5),
        'ffn_w_out': nrm(ks[9], (DEPTH, 2, D_FF, D_MODEL), D_FF ** -0.5),
        'w_in': nrm(ks[10], (DEPTH, D_MODEL, N_IN), D_MODEL ** -0.5),
        'w_out': nrm(ks[11], (DEPTH, D_MIX, D_MODEL), D_MIX ** -0.5),
        'conv_w': nrm(ks[12], (DEPTH, CONV_WIDTH, CONV_CH), CONV_WIDTH ** -0.5),
        'conv_b': nrm(ks[13], (DEPTH, CONV_CH), 0.02),
        'conv_ln_g': 1.0 + nrm(ks[14], (DEPTH, CONV_CH), 0.02),
        'conv_ln_b': nrm(ks[15], (DEPTH, CONV_CH), 0.02),
        'conv_pw': nrm(ks[16], (DEPTH, CONV_CH, CONV_CH), CONV_CH ** -0.5),
        'diff_lam': nrm(ks[17], (DEPTH, 4, DIFF_DH), 0.1),
        'diff_subln_g': 1.0 + nrm(ks[18], (DEPTH, 2 * DIFF_DH), 0.02),
        'gla_w_gate': nrm(ks[19], (DEPTH, GLA_RANK, GLA_WIDTH), GLA_RANK ** -0.5),
        'gla_b_gate': nrm(ks[20], (DEPTH, GLA_WIDTH), 0.1),
        'gla_norm_g': 1.0 + nrm(ks[21], (DEPTH, GLA_DH), 0.02),
    }


def reference(x_prompt, x_sample, cache_k, cache_v, page_table, state_conv, state_gla,
              norm_gain, ffn_w_in, ffn_w_out, w_in, w_out, conv_w, conv_b, conv_ln_g,
              conv_ln_b, conv_pw, diff_lam, diff_subln_g, gla_w_gate, gla_b_gate, gla_norm_g):
    n_pages = PAST_LEN // PAGE_SIZE
    bp = x_prompt.shape[0]
    bs = x_sample.shape[0]
    xp, xs = x_prompt, x_sample
    kp_l, vp_l, cp_l, sp_l = [], [], [], []
    ks_l, vs_l, cs_l, ss_l = [], [], [], []
    for l in range(DEPTH):
        p = {'norm_gain': norm_gain[l], 'ffn_w_in': ffn_w_in[l], 'ffn_w_out': ffn_w_out[l],
             'w_in': w_in[l], 'w_out': w_out[l], 'conv_w': conv_w[l], 'conv_b': conv_b[l],
             'conv_ln_g': conv_ln_g[l], 'conv_ln_b': conv_ln_b[l], 'conv_pw': conv_pw[l],
             'diff_lam': diff_lam[l], 'diff_subln_g': diff_subln_g[l],
             'gla_w_gate': gla_w_gate[l], 'gla_b_gate': gla_b_gate[l], 'gla_norm_g': gla_norm_g[l]}
        conv0 = jnp.zeros((bp, CONV_WIDTH - 1, CONV_CH), xp.dtype)
        s0 = jnp.zeros((bp, GLA_HEADS, GLA_DH, GLA_DH), jnp.float32)
        xp, kp, vp, cp, sp = decoder_layer(xp, p, conv0, s0, None, l)
        k_past = cache_k[l][page_table].reshape(bs, n_pages * PAGE_SIZE, DIFF_HEADS, 2 * DIFF_DH)
        v_past = cache_v[l][page_table].reshape(bs, n_pages * PAGE_SIZE, DIFF_HEADS, 2 * DIFF_DH)
        xs, ksm, vsm, csm, ssm = decoder_layer(xs, p, state_conv[l], state_gla[l], (k_past, v_past), l)
        kp_l.append(kp); vp_l.append(vp); cp_l.append(cp); sp_l.append(sp)
        ks_l.append(ksm); vs_l.append(vsm); cs_l.append(csm); ss_l.append(ssm)
    return (xp, xs, jnp.stack(kp_l), jnp.stack(vp_l), jnp.stack(cp_l), jnp.stack(sp_l),
            jnp.stack(ks_l), jnp.stack(vs_l), jnp.stack(cs_l), jnp.stack(ss_l))
```

```python
import functools
import math

import jax
import jax.numpy as jnp
from jax import lax
from jax.experimental import pallas as pl
from jax.experimental.pallas import tpu as pltpu

F32 = jnp.float32
BF16 = jnp.bfloat16
EPS = 1e-6

CONV_CH = 256
CONV_WIDTH = 31
CONV_HALO = 32
DIFF_HEADS = 4
DIFF_DH = 64
DIFF_HW = 2 * DIFF_DH
DIFF_WIDTH = DIFF_HEADS * DIFF_HW
GLA_HEADS = 4
GLA_DH = 64
GLA_WIDTH = GLA_HEADS * GLA_DH
GLA_RANK = 16
GLA_TAU = 16.0
OFF_DQ = 2 * CONV_CH
OFF_DK = OFF_DQ + DIFF_WIDTH
OFF_DV = OFF_DK + DIFF_WIDTH
OFF_GQ = OFF_DV + DIFF_WIDTH
OFF_GK = OFF_GQ + GLA_WIDTH
OFF_GV = OFF_GK + GLA_WIDTH
OFF_GG = OFF_GV + GLA_WIDTH
OFF_GA = OFF_GG + GLA_WIDTH

LANE = 128
VMEM_LIMIT = 56 * 1024 * 1024

NT_DIMS = (((1,), (1,)), ((), ()))


def _dot(a, b):
    return jnp.dot(a, b, preferred_element_type=F32)


def _dot_nt(a, b):
    return lax.dot_general(a, b, NT_DIMS, preferred_element_type=F32)


def _rms(x, g):
    return x * lax.rsqrt(jnp.mean(x * x, axis=-1, keepdims=True) + EPS) * g


def _log_sigmoid(x):
    return jnp.minimum(x, 0.0) - jnp.log1p(jnp.exp(-jnp.abs(x)))


def _split_bf16(x, parts):
    out = []
    r = x
    for _ in range(parts - 1):
        p = r.astype(BF16)
        out.append(p)
        r = r - p.astype(F32)
    out.append(r.astype(BF16))
    return out


def _params(n_axes):
    return pltpu.CompilerParams(dimension_semantics=("arbitrary",) * n_axes,
                                vmem_limit_bytes=VMEM_LIMIT)


def _resident(shape):
    nd = len(shape)
    return pl.BlockSpec(shape, lambda *_: (0,) * nd, pipeline_mode=pl.Buffered(1))


def _ffn(x, g_pre, g_post, wgu_ref, wo_ref, a_scr, d_ff, chunk):
    h = _rms(x, g_pre).astype(BF16)
    for c in range(0, d_ff, chunk):
        gate = _dot(h, wgu_ref[:, c:c + chunk])
        up = _dot(h, wgu_ref[:, d_ff + c:d_ff + c + chunk])
        a_scr[:, c:c + chunk] = (gate * jax.nn.sigmoid(gate) * up).astype(BF16)
    y = _dot(a_scr[...], wo_ref[...])
    return x + 0.5 * _rms(y, g_post)


def _dense_in_kernel(x_ref, g_ref, wgu_ref, wo_ref, win_ref, wkt_ref, wga_ref, wgat_ref,
                     wgate_ref, wgatet_ref, bg_ref, bgt_ref,
                     x1_ref, u_ref, dq_ref, kn_ref, vn_ref, kb_ref, vb_ref,
                     gq_ref, gv_ref, la_ref, gg_ref, kt_ref, lat_ref, a_scr, *, d_ff, chunk):
    g = g_ref[...]
    x1 = _ffn(x_ref[...], g[0:1], g[1:2], wgu_ref, wo_ref, a_scr, d_ff, chunk)
    x1_ref[...] = x1
    h = _rms(x1, g[2:3]).astype(BF16)

    ca = _dot(h, win_ref[:, 0:CONV_CH])
    cg = _dot(h, win_ref[:, CONV_CH:OFF_DQ])
    u_ref[...] = ca * jax.nn.sigmoid(cg)

    dq_ref[...] = (_dot(h, win_ref[:, OFF_DQ:OFF_DK]) * (DIFF_DH ** -0.5)).astype(BF16)
    k = _dot(h, win_ref[:, OFF_DK:OFF_DV])
    kn_ref[...] = k
    kb_ref[...] = k.astype(BF16)
    v = _dot(h, win_ref[:, OFF_DV:OFF_GQ])
    vn_ref[...] = v
    vb_ref[...] = v.astype(BF16)

    gq_ref[...] = _dot(h, win_ref[:, OFF_GQ:OFF_GK]) * (GLA_DH ** -0.5)
    gv_ref[...] = _dot(h, win_ref[:, OFF_GV:OFF_GG])
    gg_ref[...] = _dot(h, win_ref[:, OFF_GG:OFF_GA])
    ga = _dot(h, wga_ref[...])
    a_logit = _dot(ga.astype(BF16), wgate_ref[...]) + bg_ref[...]
    la_ref[...] = _log_sigmoid(a_logit) * (1.0 / GLA_TAU)

    kt_ref[...] = _dot_nt(wkt_ref[...], h)
    gat = _dot_nt(wgat_ref[...], h)
    a_logit_t = _dot(wgatet_ref[...], gat.astype(BF16)) + bgt_ref[...]
    lat_ref[...] = _log_sigmoid(a_logit_t) * (1.0 / GLA_TAU)


def _dense_in(x, g, wgu, wo, win, wkt, wga, wgat, wgate, wgatet, bg, bgt, *, tm):
    m, d = x.shape
    d_ff = wo.shape[0]
    chunk = 256
    assert m % tm == 0 and d_ff % chunk == 0
    row = lambda w: pl.BlockSpec((tm, w), lambda i: (i, 0))
    col = lambda h: pl.BlockSpec((h, tm), lambda i: (0, i))
    outs = [
        (jax.ShapeDtypeStruct((m, d), F32), row(d)),
        (jax.ShapeDtypeStruct((m, CONV_CH), F32), row(CONV_CH)),
        (jax.ShapeDtypeStruct((m, DIFF_WIDTH), BF16), row(DIFF_WIDTH)),
        (jax.ShapeDtypeStruct((m, DIFF_WIDTH), F32), row(DIFF_WIDTH)),
        (jax.ShapeDtypeStruct((m, DIFF_WIDTH), F32), row(DIFF_WIDTH)),
        (jax.ShapeDtypeStruct((m, DIFF_WIDTH), BF16), row(DIFF_WIDTH)),
        (jax.ShapeDtypeStruct((m, DIFF_WIDTH), BF16), row(DIFF_WIDTH)),
        (jax.ShapeDtypeStruct((m, GLA_WIDTH), F32), row(GLA_WIDTH)),
        (jax.ShapeDtypeStruct((m, GLA_WIDTH), F32), row(GLA_WIDTH)),
        (jax.ShapeDtypeStruct((m, GLA_WIDTH), F32), row(GLA_WIDTH)),
        (jax.ShapeDtypeStruct((m, GLA_WIDTH), F32), row(GLA_WIDTH)),
        (jax.ShapeDtypeStruct((GLA_WIDTH, m), F32), col(GLA_WIDTH)),
        (jax.ShapeDtypeStruct((GLA_WIDTH, m), F32), col(GLA_WIDTH)),
    ]
    weights = (g, wgu, wo, win, wkt, wga, wgat, wgate, wgatet, bg, bgt)
    return pl.pallas_call(
        functools.partial(_dense_in_kernel, d_ff=d_ff, chunk=chunk),
        grid=(m // tm,),
        in_specs=[row(d)] + [_resident(w.shape) for w in weights],
        out_specs=[s for _, s in outs],
        out_shape=[s for s, _ in outs],
        scratch_shapes=[pltpu.VMEM((tm, d_ff), BF16)],
        compiler_params=_params(1),
        name="dense_in",
    )(x, *weights)


def _dense_out_kernel(x_ref, co_ref, do_ref, go_ref, g_ref, wout_ref, wgu_ref, wo_ref,
                      x2_ref, a_scr, *, d_ff, chunk):
    g = g_ref[...]
    m = (_dot(co_ref[...], wout_ref[0:CONV_CH, :])
         + _dot(do_ref[...], wout_ref[CONV_CH:CONV_CH + DIFF_WIDTH, :])
         + _dot(go_ref[...], wout_ref[CONV_CH + DIFF_WIDTH:, :]))
    x = x_ref[...] + _rms(m, g[3:4])
    x2_ref[...] = _ffn(x, g[4:5], g[5:6], wgu_ref, wo_ref, a_scr, d_ff, chunk)


def _dense_out(x, co, do, go, g, wout, wgu, wo, *, tm):
    m, d = x.shape
    d_ff = wo.shape[0]
    chunk = 256
    assert m % tm == 0 and d_ff % chunk == 0
    row = lambda w: pl.BlockSpec((tm, w), lambda i: (i, 0))
    weights = (g, wout, wgu, wo)
    return pl.pallas_call(
        functools.partial(_dense_out_kernel, d_ff=d_ff, chunk=chunk),
        grid=(m // tm,),
        in_specs=[row(d), row(CONV_CH), row(DIFF_WIDTH), row(GLA_WIDTH)]
        + [_resident(w.shape) for w in weights],
        out_specs=row(d),
        out_shape=jax.ShapeDtypeStruct((m, d), F32),
        scratch_shapes=[pltpu.VMEM((tm, d_ff), BF16)],
        compiler_params=_params(1),
        name="dense_out",
    )(x, co, do, go, *weights)


def _conv_kernel(u_ref, buf_ref, w_ref, b_ref, lg_ref, lb_ref, pw_ref, o_ref, nc_ref, scr,
                 *, t, tt):
    hist = CONV_WIDTH - 1
    lead = CONV_HALO - hist
    scr[0:lead, :] = jnp.zeros((lead, CONV_CH), F32)
    scr[lead:CONV_HALO, :] = buf_ref[...]
    scr[CONV_HALO:CONV_HALO + t, :] = u_ref[...]
    nc_ref[...] = scr[t + lead:t + CONV_HALO, :]
    w = w_ref[...]
    bias = b_ref[...]
    lg = lg_ref[...]
    lb = lb_ref[...]
    pw = pw_ref[...]

    def tile(i, carry):
        t0 = pl.multiple_of(i * tt, tt)
        win = scr[pl.ds(t0, tt + CONV_HALO), :]
        acc = jnp.zeros((tt, CONV_CH), F32) + bias
        for r in range(8):
            taps = [j for j in range(CONV_WIDTH) if (lead + j) % 8 == r]
            sh = win if r == 0 else win[r:r + tt + CONV_HALO - 8, :]
            for j in taps:
                a = lead + j - r
                acc = acc + w[j:j + 1, :] * sh[a:a + tt, :]
        mu = jnp.mean(acc, axis=-1, keepdims=True)
        xc = acc - mu
        y = xc * lax.rsqrt(jnp.mean(xc * xc, axis=-1, keepdims=True) + EPS) * lg + lb
        y = y * jax.nn.sigmoid(y)
        o_ref[pl.ds(t0, tt), :] = _dot(y.astype(BF16), pw).astype(BF16)
        return carry

    lax.fori_loop(0, t // tt, tile, 0)


def _conv(u, buf, w, b, lg, lb, pw, *, nb, t):
    tt = min(t, 256)
    assert t % tt == 0 and tt % 8 == 0
    hist = CONV_WIDTH - 1
    u3 = u.reshape(nb, t, CONV_CH)
    small = lambda a: _resident(a.shape)
    out, new_conv = pl.pallas_call(
        functools.partial(_conv_kernel, t=t, tt=tt),
        grid=(nb,),
        in_specs=[pl.BlockSpec((None, t, CONV_CH), lambda i: (i, 0, 0)),
                  pl.BlockSpec((None, hist, CONV_CH), lambda i: (i, 0, 0)),
                  small(w), small(b), small(lg), small(lb), small(pw)],
        out_specs=[pl.BlockSpec((None, t, CONV_CH), lambda i: (i, 0, 0)),
                   pl.BlockSpec((None, hist, CONV_CH), lambda i: (i, 0, 0))],
        out_shape=[jax.ShapeDtypeStruct((nb, t, CONV_CH), BF16),
                   jax.ShapeDtypeStruct((nb, hist, CONV_CH), F32)],
        scratch_shapes=[pltpu.VMEM((CONV_HALO + t, CONV_CH), F32)],
        compiler_params=_params(1),
        name="conv",
    )(u3, buf, w, b, lg, lb, pw)
    return out.reshape(nb * t, CONV_CH), new_conv


def _lam(lam_ref, lam_init):
    dl = lam_ref[...]
    s1 = jnp.sum(dl[0:1] * dl[1:2], axis=-1, keepdims=True)
    s2 = jnp.sum(dl[2:3] * dl[3:4], axis=-1, keepdims=True)
    return jnp.exp(s1) - jnp.exp(s2) + lam_init


def _attn_p_kernel(q_ref, k_ref, v_ref, lam_ref, sg_ref, o_ref, m_scr, l_scr, acc_scr,
                   *, tq, lam_init):
    qi = pl.program_id(2)
    q = q_ref[...].astype(F32)
    lane = lax.broadcasted_iota(jnp.int32, (tq, DIFF_HW), 1)
    qs = jnp.concatenate([jnp.where(lane < DIFF_DH, q, 0.0),
                          jnp.where(lane >= DIFF_DH, q, 0.0)], axis=0).astype(BF16)
    m_scr[...] = jnp.full(m_scr.shape, -jnp.inf, F32)
    l_scr[...] = jnp.zeros(l_scr.shape, F32)
    acc_scr[...] = jnp.zeros(acc_scr.shape, F32)

    def block(j, masked):
        k0 = pl.multiple_of(j * tq, tq)
        s = _dot_nt(qs, k_ref[pl.ds(k0, tq), :])
        if masked:
            r = lax.broadcasted_iota(jnp.int32, (2 * tq, tq), 0)
            c = lax.broadcasted_iota(jnp.int32, (2 * tq, tq), 1)
            r = jnp.where(r >= tq, r - tq, r)
            s = jnp.where(c <= r, s, -jnp.inf)
        m_prev = m_scr[...]
        m_new = jnp.maximum(m_prev, jnp.max(s, axis=-1, keepdims=True))
        alpha = jnp.exp(m_prev - m_new)
        p = jnp.exp(s - m_new[:, 0:1])
        l_scr[...] = alpha * l_scr[...] + jnp.sum(p, axis=-1, keepdims=True)
        acc_scr[...] = alpha * acc_scr[...] + _dot(p.astype(BF16), v_ref[pl.ds(k0, tq), :])
        m_scr[...] = m_new

    def body(j, carry):
        block(j, False)
        return carry

    lax.fori_loop(0, qi, body, 0)
    block(qi, True)

    o = acc_scr[...] / l_scr[...]
    lam = _lam(lam_ref, lam_init)
    o = o[0:tq] - lam * o[tq:2 * tq]
    o_ref[...] = (_rms(o, sg_ref[...]) * (1.0 - lam_init)).astype(BF16)


def _attn_prompt(dq, kb, vb, lam_p, sg, *, nb, t, lam_init):
    tq = 512
    assert t % tq == 0
    nq = t // tq
    return pl.pallas_call(
        functools.partial(_attn_p_kernel, tq=tq, lam_init=lam_init),
        grid=(nb, DIFF_HEADS, nq),
        in_specs=[pl.BlockSpec((tq, DIFF_HW), lambda b, h, i: (b * nq + i, h)),
                  pl.BlockSpec((t, DIFF_HW), lambda b, h, i: (b, h)),
                  pl.BlockSpec((t, DIFF_HW), lambda b, h, i: (b, h)),
                  _resident(lam_p.shape), _resident(sg.shape)],
        out_specs=pl.BlockSpec((tq, DIFF_HW), lambda b, h, i: (b * nq + i, h)),
        out_shape=jax.ShapeDtypeStruct((nb * t, DIFF_WIDTH), BF16),
        scratch_shapes=[pltpu.VMEM((2 * tq, DIFF_HW), F32),
                        pltpu.VMEM((2 * tq, DIFF_HW), F32),
                        pltpu.VMEM((2 * tq, DIFF_HW), F32)],
        compiler_params=_params(3),
        name="attn_prompt",
    )(dq, kb, vb, lam_p, sg)


def _attn_s_kernel(pt_ref, q_ref, kn_ref, vn_ref, lam_ref, sg_ref, ck_hbm, cv_hbm, o_ref,
                   kbuf, vbuf, sem, *, layer, n_pages, page, ts, lam_init):
    b = pl.program_id(0)
    nb = pl.num_programs(0)
    slot = lax.rem(b, 2)
    past = n_pages * page
    rows = kbuf.shape[1]

    def page_copies(bb, sl):
        cps = []
        for p in range(n_pages):
            pg = pt_ref[bb, p]
            dst = pl.ds(p * page, page)
            cps.append(pltpu.make_async_copy(ck_hbm.at[layer, pg], kbuf.at[sl, dst], sem.at[0, sl]))
            cps.append(pltpu.make_async_copy(cv_hbm.at[layer, pg], vbuf.at[sl, dst], sem.at[1, sl]))
        return cps

    @pl.when(b == 0)
    def _():
        for sl in range(2):
            kbuf[sl, past:rows, :] = jnp.zeros((rows - past, DIFF_WIDTH), F32)
            vbuf[sl, past:rows, :] = jnp.zeros((rows - past, DIFF_WIDTH), F32)
        for cp in page_copies(0, 0):
            cp.start()

    @pl.when(b + 1 < nb)
    def _():
        for cp in page_copies(b + 1, 1 - slot):
            cp.start()

    for cp in page_copies(b, slot):
        cp.wait()

    kbuf[slot, past:past + ts, :] = kn_ref[...]
    vbuf[slot, past:past + ts, :] = vn_ref[...]

    q = q_ref[...].astype(F32)
    ngrp = 2 * DIFF_HEADS
    grp = lax.broadcasted_iota(jnp.int32, (ts, DIFF_WIDTH), 1) // DIFF_DH
    qbd = jnp.concatenate([jnp.where(grp == r, q, 0.0) for r in range(ngrp)], axis=0).astype(BF16)
    s = _dot_nt(qbd, kbuf[slot].astype(BF16))
    col = lax.broadcasted_iota(jnp.int32, (ngrp * ts, rows), 1)
    qpos = lax.rem(lax.broadcasted_iota(jnp.int32, (ngrp * ts, rows), 0), ts)
    s = jnp.where(col - past <= qpos, s, -jnp.inf)
    m = jnp.max(s, axis=-1, keepdims=True)
    p = jnp.exp(s - m)
    l = jnp.sum(p, axis=-1, keepdims=True)
    pv = _dot(p.astype(BF16), vbuf[slot].astype(BF16)) / l
    lam = _lam(lam_ref, lam_init)
    sg = sg_ref[...]
    outs = []
    for h in range(DIFF_HEADS):
        lanes = slice(h * DIFF_HW, (h + 1) * DIFF_HW)
        o = pv[2 * h * ts:(2 * h + 1) * ts, lanes] - lam * pv[(2 * h + 1) * ts:(2 * h + 2) * ts, lanes]
        outs.append(_rms(o, sg) * (1.0 - lam_init))
    o_ref[...] = jnp.concatenate(outs, axis=-1).astype(BF16)


def _attn_sample(page_table, dq, kn, vn, lam_p, sg, ck, cv, *, layer, nb, ts, lam_init):
    n_pages = page_table.shape[1]
    page = ck.shape[2]
    past = n_pages * page
    rows = past + LANE
    assert ts <= LANE and ts % 8 == 0
    tok = lambda w: pl.BlockSpec((ts, w), lambda b, pt: (b, 0))
    const = lambda a: pl.BlockSpec(a.shape, lambda b, pt: (0,) * a.ndim, pipeline_mode=pl.Buffered(1))
    grid_spec = pltpu.PrefetchScalarGridSpec(
        num_scalar_prefetch=1,
        grid=(nb,),
        in_specs=[tok(DIFF_WIDTH), tok(DIFF_WIDTH), tok(DIFF_WIDTH), const(lam_p), const(sg),
                  pl.BlockSpec(memory_space=pl.ANY), pl.BlockSpec(memory_space=pl.ANY)],
        out_specs=tok(DIFF_WIDTH),
        scratch_shapes=[pltpu.VMEM((2, rows, DIFF_WIDTH), F32),
                        pltpu.VMEM((2, rows, DIFF_WIDTH), F32),
                        pltpu.SemaphoreType.DMA((2, 2))],
    )
    return pl.pallas_call(
        functools.partial(_attn_s_kernel, layer=layer, n_pages=n_pages, page=page, ts=ts,
                          lam_init=lam_init),
        grid_spec=grid_spec,
        out_shape=jax.ShapeDtypeStruct((nb * ts, DIFF_WIDTH), BF16),
        compiler_params=_params(1),
        name="attn_sample",
    )(page_table, dq, kn, vn, lam_p, sg, ck, cv)


def _gla_kernel(q_ref, v_ref, la_ref, gg_ref, kt_ref, lat_ref, s0_ref, gn_ref,
                o_ref, sout_ref, s_scr, *, c, groups):
    step = pl.program_id(1)
    r = c // groups

    @pl.when(step == 0)
    def _():
        s_scr[...] = s0_ref[...]

    row = lax.broadcasted_iota(jnp.int32, (c, c), 0)
    col = lax.broadcasted_iota(jnp.int32, (c, c), 1)
    if groups > 1:
        same = (row // r) == (col // r)
        causal = jnp.logical_and(col <= row, same)
        causal_t = jnp.logical_and(row <= col, same)
    else:
        causal = col <= row
        causal_t = row <= col
    ltri = jnp.where(causal, 1.0, 0.0).astype(BF16)
    utri = jnp.where(causal_t, 1.0, 0.0).astype(BF16)

    cum = sum(_dot(ltri, part) for part in _split_bf16(la_ref[...], 3))
    cum_t = sum(_dot(part, utri) for part in _split_bf16(lat_ref[...], 3))

    q = q_ref[...]
    kt = kt_ref[...]
    v = v_ref[...]
    vb = v.astype(BF16)
    qd = q * jnp.exp(cum)
    if groups == 1:
        qt = q * jnp.exp(cum - cum[c // 2:c // 2 + 1, :])
        ktt = kt * jnp.exp(cum_t[:, c // 2:c // 2 + 1] - cum_t)
    else:
        qt = qd
        ktt = kt * jnp.exp(-cum_t)
    kttb = ktt.astype(BF16)

    head_l = lax.broadcasted_iota(jnp.int32, (c, GLA_WIDTH), 1) // GLA_DH
    o = jnp.zeros((c, GLA_WIDTH), F32)
    for h in range(GLA_HEADS):
        mh = head_l == h
        att = _dot(jnp.where(mh, qt, 0.0).astype(BF16), kttb)
        att = jnp.where(causal, att, 0.0)
        o = o + _dot(att.astype(BF16), jnp.where(mh, v, 0.0).astype(BF16))

    hr = lax.broadcasted_iota(jnp.int32, (GLA_WIDTH, GLA_WIDTH), 0) // GLA_DH
    hc = lax.broadcasted_iota(jnp.int32, (GLA_WIDTH, GLA_WIDTH), 1) // GLA_DH
    same_head = hr == hc
    colg = lax.broadcasted_iota(jnp.int32, (GLA_WIDTH, c), 1) // r
    inter = []
    for gi in range(groups):
        s_old = s_scr[gi]
        inter.append(_dot(qd[gi * r:(gi + 1) * r, :].astype(BF16), s_old.astype(BF16)))
        last_t = cum_t[:, (gi + 1) * r - 1:(gi + 1) * r]
        if groups > 1:
            ing = colg == gi
            kd_t = jnp.where(ing, kt * jnp.exp(jnp.where(ing, last_t - cum_t, 0.0)), 0.0)
        else:
            kd_t = kt * jnp.exp(last_t - cum_t)
        upd = _dot(kd_t.astype(BF16), vb)
        s_scr[gi] = s_old * jnp.exp(last_t) + jnp.where(same_head, upd, 0.0)
    o = o + (inter[0] if groups == 1 else jnp.concatenate(inter, axis=0))

    avg = jnp.where(same_head, 1.0 / GLA_DH, 0.0).astype(BF16)
    msq = sum(_dot(part, avg) for part in _split_bf16(o * o, 2))
    gg = gg_ref[...]
    o_ref[...] = (o * lax.rsqrt(msq + EPS) * gn_ref[...] * (gg * jax.nn.sigmoid(gg))).astype(BF16)

    @pl.when(step == pl.num_programs(1) - 1)
    def _():
        sout_ref[...] = s_scr[...]


def _gla(gq, gv, la, gg, kt, lat, s0, gn, *, nseq, t):
    c = LANE
    if t >= c:
        assert t % c == 0
        groups, nb, nsteps = 1, nseq, t // c
    else:
        assert c % t == 0 and nseq % (c // t) == 0
        groups, nb, nsteps = c // t, nseq // (c // t), 1
    m = nseq * t
    row = pl.BlockSpec((c, GLA_WIDTH), lambda b, s: (b * nsteps + s, 0))
    colb = pl.BlockSpec((GLA_WIDTH, c), lambda b, s: (0, b * nsteps + s))
    st = pl.BlockSpec((groups, GLA_WIDTH, GLA_WIDTH), lambda b, s: (b, 0, 0))
    return pl.pallas_call(
        functools.partial(_gla_kernel, c=c, groups=groups),
        grid=(nb, nsteps),
        in_specs=[row, row, row, row, colb, colb, st,
                  pl.BlockSpec(gn.shape, lambda b, s: (0, 0))],
        out_specs=[row, st],
        out_shape=[jax.ShapeDtypeStruct((m, GLA_WIDTH), BF16),
                   jax.ShapeDtypeStruct((nseq, GLA_WIDTH, GLA_WIDTH), F32)],
        scratch_shapes=[pltpu.VMEM((groups, GLA_WIDTH, GLA_WIDTH), F32)],
        compiler_params=_params(2),
        name="gla",
    )(gq, gv, la, gg, kt, lat, s0, gn)


def _state_to_blockdiag(s):
    n = s.shape[0]
    out = jnp.zeros((n, GLA_WIDTH, GLA_WIDTH), F32)
    for h in range(GLA_HEADS):
        out = lax.dynamic_update_slice(out, s[:, h].astype(F32), (0, h * GLA_DH, h * GLA_DH))
    return out


def _state_from_blockdiag(s):
    return jnp.stack([s[:, h * GLA_DH:(h + 1) * GLA_DH, h * GLA_DH:(h + 1) * GLA_DH]
                      for h in range(GLA_HEADS)], axis=1)


def _layer_weights(l, norm_gain, ffn_w_in, ffn_w_out, w_in, w_out, conv_w, conv_b, conv_ln_g,
                   conv_ln_b, conv_pw, diff_lam, diff_subln_g, gla_w_gate, gla_b_gate, gla_norm_g):
    pad_r = LANE - GLA_RANK
    wga = jnp.pad(w_in[l][:, OFF_GA:], ((0, 0), (0, pad_r))).astype(BF16)
    wgate = jnp.pad(gla_w_gate[l], ((0, pad_r), (0, 0))).astype(BF16)
    return dict(
        g=norm_gain[l],
        wgu=[ffn_w_in[l, i].astype(BF16) for i in range(2)],
        wo=[ffn_w_out[l, i].astype(BF16) for i in range(2)],
        win=w_in[l][:, :OFF_GA].astype(BF16),
        wkt=w_in[l][:, OFF_GK:OFF_GV].T.astype(BF16),
        wga=wga, wgat=wga.T, wgate=wgate, wgatet=wgate.T,
        bg=gla_b_gate[l].reshape(1, GLA_WIDTH), bgt=gla_b_gate[l].reshape(GLA_WIDTH, 1),
        wout=w_out[l].astype(BF16),
        conv_w=conv_w[l], conv_b=conv_b[l].reshape(1, CONV_CH),
        conv_lg=conv_ln_g[l].reshape(1, CONV_CH), conv_lb=conv_ln_b[l].reshape(1, CONV_CH),
        conv_pw=conv_pw[l].astype(BF16),
        lam=diff_lam[l], sg=diff_subln_g[l].reshape(1, DIFF_HW),
        gn=jnp.tile(gla_norm_g[l], GLA_HEADS).reshape(1, GLA_WIDTH),
    )


def kernel(x_prompt, x_sample, cache_k, cache_v, page_table, state_conv, state_gla, norm_gain,
           ffn_w_in, ffn_w_out, w_in, w_out, conv_w, conv_b, conv_ln_g, conv_ln_b, conv_pw,
           diff_lam, diff_subln_g, gla_w_gate, gla_b_gate, gla_norm_g):
    bp, tp, d = x_prompt.shape
    bs, ts, _ = x_sample.shape
    depth = norm_gain.shape[0]
    n_pool, page = cache_k.shape[1], cache_k.shape[2]
    ck = cache_k.reshape(depth, n_pool, page, DIFF_WIDTH)
    cv = cache_v.reshape(depth, n_pool, page, DIFF_WIDTH)

    xp = x_prompt.reshape(bp * tp, d)
    xs = x_sample.reshape(bs * ts, d)
    conv0 = jnp.zeros((bp, CONV_WIDTH - 1, CONV_CH), F32)
    s0p = jnp.zeros((bp, GLA_WIDTH, GLA_WIDTH), F32)
    outs = [[] for _ in range(8)]
    for l in range(depth):
        w = _layer_weights(l, norm_gain, ffn_w_in, ffn_w_out, w_in, w_out, conv_w, conv_b,
                           conv_ln_g, conv_ln_b, conv_pw, diff_lam, diff_subln_g, gla_w_gate,
                           gla_b_gate, gla_norm_g)
        lam_init = 0.8 - 0.6 * math.exp(-0.3 * l)

        def dense_in(x, tm):
            return _dense_in(x, w["g"], w["wgu"][0], w["wo"][0], w["win"], w["wkt"], w["wga"],
                             w["wgat"], w["wgate"], w["wgatet"], w["bg"], w["bgt"], tm=tm)

        def conv(u, buf, nb, t):
            return _conv(u, buf, w["conv_w"], w["conv_b"], w["conv_lg"], w["conv_lb"],
                         w["conv_pw"], nb=nb, t=t)

        def dense_out(x, co, do, go, tm):
            return _dense_out(x, co, do, go, w["g"], w["wout"], w["wgu"][1], w["wo"][1], tm=tm)

        x1, u, dq, kn, vn, kb, vb, gq, gv, la, gg, kt, lat = dense_in(xp, 256)
        co, ncp = conv(u, conv0, bp, tp)
        do = _attn_prompt(dq, kb, vb, w["lam"], w["sg"], nb=bp, t=tp, lam_init=lam_init)
        go, sp = _gla(gq, gv, la, gg, kt, lat, s0p, w["gn"], nseq=bp, t=tp)
        xp = dense_out(x1, co, do, go, 512)
        outs[0].append(kn.reshape(bp, tp, DIFF_HEADS, DIFF_HW))
        outs[1].append(vn.reshape(bp, tp, DIFF_HEADS, DIFF_HW))
        outs[2].append(ncp)
        outs[3].append(_state_from_blockdiag(sp))

        x1, u, dq, kn, vn, kb, vb, gq, gv, la, gg, kt, lat = dense_in(xs, 256)
        co, ncs = conv(u, state_conv[l], bs, ts)
        do = _attn_sample(page_table, dq, kn, vn, w["lam"], w["sg"], ck, cv, layer=l, nb=bs,
                          ts=ts, lam_init=lam_init)
        go, ss = _gla(gq, gv, la, gg, kt, lat, _state_to_blockdiag(state_gla[l]), w["gn"],
                      nseq=bs, t=ts)
        xs = dense_out(x1, co, do, go, 512)
        outs[4].append(kn.reshape(bs, ts, DIFF_HEADS, DIFF_HW))
        outs[5].append(vn.reshape(bs, ts, DIFF_HEADS, DIFF_HW))
        outs[6].append(ncs)
        outs[7].append(_state_from_blockdiag(ss))

    return (xp.reshape(bp, tp, d), xs.reshape(bs, ts, d)) + tuple(jnp.stack(o) for o in outs)
```

```python
import functools
import math

import jax
import jax.numpy as jnp
from jax import lax
from jax.experimental import pallas as pl
from jax.experimental.pallas import tpu as pltpu

F32 = jnp.float32
BF16 = jnp.bfloat16
EPS = 1e-6
LOG2E = math.log2(math.e)

CONV_CH = 256
CONV_WIDTH = 31
CONV_HALO = 32
DIFF_HEADS = 4
DIFF_DH = 64
DIFF_HW = 2 * DIFF_DH
DIFF_WIDTH = DIFF_HEADS * DIFF_HW
GLA_HEADS = 4
GLA_DH = 64
GLA_WIDTH = GLA_HEADS * GLA_DH
GLA_RANK = 16
GLA_TAU = 16.0
OFF_DQ = 2 * CONV_CH
OFF_DK = OFF_DQ + DIFF_WIDTH
OFF_DV = OFF_DK + DIFF_WIDTH
OFF_GQ = OFF_DV + DIFF_WIDTH
OFF_GK = OFF_GQ + GLA_WIDTH
OFF_GV = OFF_GK + GLA_WIDTH
OFF_GG = OFF_GV + GLA_WIDTH
OFF_GA = OFF_GG + GLA_WIDTH

LANE = 128
SUBLANE = 8
MXU_N = 256
ATTN_ROWS = 64
VMEM_LIMIT = 56 * 1024 * 1024

NT_DIMS = (((1,), (1,)), ((), ()))


def _dot(a, b):
    return jnp.dot(a, b, preferred_element_type=F32)


def _dot_nt(a, b):
    return lax.dot_general(a, b, NT_DIMS, preferred_element_type=F32)


def _rms(x, g):
    return x * lax.rsqrt(jnp.mean(x * x, axis=-1, keepdims=True) + EPS) * g


def _log_sigmoid(x):
    return jnp.minimum(x, 0.0) - jnp.log1p(jnp.exp(-jnp.abs(x)))


def _split_bf16(x, parts):
    out = []
    r = x
    for _ in range(parts - 1):
        p = r.astype(BF16)
        out.append(p)
        r = r - p.astype(F32)
    out.append(r.astype(BF16))
    return out


def _dot_split(x, w01, parts=3):
    return sum(_dot(p, w01) for p in _split_bf16(x, parts))


def _params(n_axes):
    return pltpu.CompilerParams(dimension_semantics=("arbitrary",) * n_axes,
                                vmem_limit_bytes=VMEM_LIMIT)


def _resident(shape):
    nd = len(shape)
    return pl.BlockSpec(shape, lambda *_: (0,) * nd, pipeline_mode=pl.Buffered(1))


def _ffn(x, g_pre, g_post, wgu_ref, wo_ref, a_scr, d_ff, chunk):
    h = _rms(x, g_pre).astype(BF16)
    for c in range(0, d_ff, chunk):
        gate = _dot(h, wgu_ref[:, c:c + chunk])
        up = _dot(h, wgu_ref[:, d_ff + c:d_ff + c + chunk])
        a_scr[:, c:c + chunk] = (gate * jax.nn.sigmoid(gate) * up).astype(BF16)
    y = _dot(a_scr[...], wo_ref[...])
    return x + 0.5 * _rms(y, g_post)


def _dense_in_kernel(*refs, d_ff, chunk, prompt):
    if prompt:
        (x_ref, g_ref, wgu_ref, wo_ref, win_ref, wkt_ref, wga_ref, wgat_ref, wgate_ref, wgatet_ref,
         bg_ref, bgt_ref, wdqt_ref, wvt_ref,
         x1_ref, u_ref, kn_ref, vn_ref, gq_ref, gv_ref, la_ref, gg_ref, kt_ref, lat_ref,
         dqt_ref, kb_ref, vt_ref, a_scr) = refs
    else:
        (x_ref, g_ref, wgu_ref, wo_ref, win_ref, wkt_ref, wga_ref, wgat_ref, wgate_ref, wgatet_ref,
         bg_ref, bgt_ref,
         x1_ref, u_ref, kn_ref, vn_ref, gq_ref, gv_ref, la_ref, gg_ref, kt_ref, lat_ref,
         dq_ref, a_scr) = refs
    g = g_ref[...]
    x1 = _ffn(x_ref[...], g[0:1], g[1:2], wgu_ref, wo_ref, a_scr, d_ff, chunk)
    x1_ref[...] = x1
    h = _rms(x1, g[2:3]).astype(BF16)

    ca = _dot(h, win_ref[:, 0:CONV_CH])
    cg = _dot(h, win_ref[:, CONV_CH:OFF_DQ])
    u_ref[...] = ca * jax.nn.sigmoid(cg)

    k = _dot(h, win_ref[:, OFF_DK:OFF_DV])
    kn_ref[...] = k
    vn_ref[...] = _dot(h, win_ref[:, OFF_DV:OFF_GQ])
    if prompt:
        dqt_ref[...] = (_dot_nt(wdqt_ref[...], h) * (DIFF_DH ** -0.5 * LOG2E)).astype(BF16)
        kb_ref[...] = k.astype(BF16)
        vt_ref[...] = _dot_nt(wvt_ref[...], h).astype(BF16)
    else:
        dq_ref[...] = (_dot(h, win_ref[:, OFF_DQ:OFF_DK]) * (DIFF_DH ** -0.5)).astype(BF16)

    gq_ref[...] = _dot(h, win_ref[:, OFF_GQ:OFF_GK]) * (GLA_DH ** -0.5)
    gv_ref[...] = _dot(h, win_ref[:, OFF_GV:OFF_GG])
    gg_ref[...] = _dot(h, win_ref[:, OFF_GG:OFF_GA])
    ga = _dot(h, wga_ref[...])
    a_logit = _dot(ga.astype(BF16), wgate_ref[...]) + bg_ref[...]
    la_ref[...] = _log_sigmoid(a_logit) * (1.0 / GLA_TAU)

    kt_ref[...] = _dot_nt(wkt_ref[...], h)
    gat = _dot_nt(wgat_ref[...], h)
    a_logit_t = _dot(wgatet_ref[...], gat.astype(BF16)) + bgt_ref[...]
    lat_ref[...] = _log_sigmoid(a_logit_t) * (1.0 / GLA_TAU)


def _dense_in(x, weights, *, tm, prompt):
    m, d = x.shape
    d_ff = weights[2].shape[0]
    chunk = MXU_N
    assert m % tm == 0 and d_ff % chunk == 0
    row = lambda w: pl.BlockSpec((tm, w), lambda i: (i, 0))
    col = lambda h: pl.BlockSpec((h, tm), lambda i: (0, i))
    rows = lambda w, dt: (jax.ShapeDtypeStruct((m, w), dt), row(w))
    cols = lambda h, dt: (jax.ShapeDtypeStruct((h, m), dt), col(h))
    outs = [rows(d, F32),
            rows(CONV_CH, F32),
            rows(DIFF_WIDTH, F32), rows(DIFF_WIDTH, F32),
            rows(GLA_WIDTH, F32), rows(GLA_WIDTH, F32),
            rows(GLA_WIDTH, F32), rows(GLA_WIDTH, F32),
            cols(GLA_WIDTH, F32), cols(GLA_WIDTH, F32)]
    if prompt:
        outs += [cols(DIFF_WIDTH, BF16), rows(DIFF_WIDTH, BF16), cols(DIFF_WIDTH, BF16)]
    else:
        outs += [rows(DIFF_WIDTH, BF16)]
    return pl.pallas_call(
        functools.partial(_dense_in_kernel, d_ff=d_ff, chunk=chunk, prompt=prompt),
        grid=(m // tm,),
        in_specs=[row(d)] + [_resident(w.shape) for w in weights],
        out_specs=[s for _, s in outs],
        out_shape=[s for s, _ in outs],
        scratch_shapes=[pltpu.VMEM((tm, d_ff), BF16)],
        compiler_params=_params(1),
        name="dense_in",
    )(x, *weights)


def _dense_out_kernel(x_ref, co_ref, do_ref, go_ref, g_ref, wout_ref, wgu_ref, wo_ref,
                      x2_ref, a_scr, *, d_ff, chunk):
    g = g_ref[...]
    m = (_dot(co_ref[...], wout_ref[0:CONV_CH, :])
         + _dot(do_ref[...], wout_ref[CONV_CH:CONV_CH + DIFF_WIDTH, :])
         + _dot(go_ref[...], wout_ref[CONV_CH + DIFF_WIDTH:, :]))
    x = x_ref[...] + _rms(m, g[3:4])
    x2_ref[...] = _ffn(x, g[4:5], g[5:6], wgu_ref, wo_ref, a_scr, d_ff, chunk)


def _dense_out(x, co, do, go, g, wout, wgu, wo, *, tm):
    m, d = x.shape
    d_ff = wo.shape[0]
    chunk = MXU_N
    assert m % tm == 0 and d_ff % chunk == 0
    row = lambda w: pl.BlockSpec((tm, w), lambda i: (i, 0))
    weights = (g, wout, wgu, wo)
    return pl.pallas_call(
        functools.partial(_dense_out_kernel, d_ff=d_ff, chunk=chunk),
        grid=(m // tm,),
        in_specs=[row(d), row(CONV_CH), row(DIFF_WIDTH), row(GLA_WIDTH)]
        + [_resident(w.shape) for w in weights],
        out_specs=row(d),
        out_shape=jax.ShapeDtypeStruct((m, d), F32),
        scratch_shapes=[pltpu.VMEM((tm, d_ff), BF16)],
        compiler_params=_params(1),
        name="dense_out",
    )(x, co, do, go, *weights)


def _conv_kernel(u_ref, buf_ref, w_ref, b_ref, lg_ref, lb_ref, pw_ref, o_ref, nc_ref, scr,
                 *, nseq, t, tt):
    hist = CONV_WIDTH - 1
    lead = CONV_HALO - hist
    w = w_ref[...]
    bias = b_ref[...]
    lg = lg_ref[...]
    lb = lb_ref[...]
    pw = pw_ref[...]
    scr[0:lead, :] = jnp.zeros((lead, CONV_CH), F32)

    for s in range(nseq):
        scr[lead:CONV_HALO, :] = buf_ref[s]
        scr[CONV_HALO:CONV_HALO + t, :] = u_ref[s]
        nc_ref[s] = scr[t + lead:t + CONV_HALO, :]

        def tile(i, carry):
            t0 = pl.multiple_of(i * tt, tt)
            win = scr[pl.ds(t0, tt + CONV_HALO), :]
            acc = jnp.zeros((tt, CONV_CH), F32) + bias
            for r in range(SUBLANE):
                taps = [j for j in range(CONV_WIDTH) if (lead + j) % SUBLANE == r]
                sh = win if r == 0 else win[r:r + tt + CONV_HALO - SUBLANE, :]
                for j in taps:
                    a = lead + j - r
                    acc = acc + w[j:j + 1, :] * sh[a:a + tt, :]
            mu = jnp.mean(acc, axis=-1, keepdims=True)
            xc = acc - mu
            y = xc * lax.rsqrt(jnp.mean(xc * xc, axis=-1, keepdims=True) + EPS) * lg + lb
            y = y * jax.nn.sigmoid(y)
            o_ref[s, pl.ds(t0, tt), :] = _dot(y.astype(BF16), pw).astype(BF16)
            return carry

        lax.fori_loop(0, t // tt, tile, 0)


def _conv(u, buf, w, b, lg, lb, pw, *, nb, t):
    tt = min(t, 256)
    nseq = 1 if t >= 256 else min(nb, 16)
    assert t % tt == 0 and tt % SUBLANE == 0 and nb % nseq == 0
    hist = CONV_WIDTH - 1
    u3 = u.reshape(nb, t, CONV_CH)
    small = lambda a: _resident(a.shape)
    out, new_conv = pl.pallas_call(
        functools.partial(_conv_kernel, nseq=nseq, t=t, tt=tt),
        grid=(nb // nseq,),
        in_specs=[pl.BlockSpec((nseq, t, CONV_CH), lambda i: (i, 0, 0)),
                  pl.BlockSpec((nseq, hist, CONV_CH), lambda i: (i, 0, 0)),
                  small(w), small(b), small(lg), small(lb), small(pw)],
        out_specs=[pl.BlockSpec((nseq, t, CONV_CH), lambda i: (i, 0, 0)),
                   pl.BlockSpec((nseq, hist, CONV_CH), lambda i: (i, 0, 0))],
        out_shape=[jax.ShapeDtypeStruct((nb, t, CONV_CH), BF16),
                   jax.ShapeDtypeStruct((nb, hist, CONV_CH), F32)],
        scratch_shapes=[pltpu.VMEM((CONV_HALO + t, CONV_CH), F32)],
        compiler_params=_params(1),
        name="conv",
    )(u3, buf, w, b, lg, lb, pw)
    return out.reshape(nb * t, CONV_CH), new_conv


def _fold_rows(x, op):
    n = x.shape[0]
    while n > SUBLANE:
        n //= 2
        x = op(x[:n], x[n:2 * n])
    return x


def _fold_list(xs, op):
    while len(xs) > 1:
        xs = [op(xs[i], xs[i + 1]) for i in range(0, len(xs) - 1, 2)] + (xs[-1:] if len(xs) % 2 else [])
    return xs[0]


def _lam(lam_ref, lam_init):
    dl = lam_ref[...]
    s1 = jnp.sum(dl[0:1] * dl[1:2], axis=-1, keepdims=True)
    s2 = jnp.sum(dl[2:3] * dl[3:4], axis=-1, keepdims=True)
    return jnp.exp(s1) - jnp.exp(s2) + lam_init


def _attn_p_kernel(qt_ref, k_ref, vt_ref, lam_ref, sg_ref, o_ref, m_scr, l_scr, acc_scr,
                   qs_scr, sa_scr, sb_scr, *, tq, lam_init):
    qi = pl.program_id(2)
    qt = qt_ref[...].astype(F32)
    sub = lax.broadcasted_iota(jnp.int32, (DIFF_HW, tq), 0)
    qs = jnp.concatenate([jnp.where(sub < DIFF_DH, qt, 0.0),
                          jnp.where(sub >= DIFF_DH, qt, 0.0)], axis=1).astype(BF16)
    m_scr[...] = jnp.full(m_scr.shape, -jnp.inf, F32)
    l_scr[...] = jnp.zeros(l_scr.shape, F32)
    acc_scr[...] = jnp.zeros(acc_scr.shape, F32)

    qs_scr[...] = qs
    tiles = [slice(c * MXU_N, (c + 1) * MXU_N) for c in range(2 * tq // MXU_N)]

    def scores(j, s_ref):
        k0 = pl.multiple_of(j * tq, tq)
        kb = k_ref[pl.ds(k0, tq), :]
        for cs in tiles:
            s_ref[:, cs] = _dot(kb, qs_scr[:, cs])

    def softmax_pv(j, s_ref, masked):
        k0 = pl.multiple_of(j * tq, tq)
        vtb = vt_ref[:, pl.ds(k0, tq)]
        chunks = range(0, tq, ATTN_ROWS)
        for cs in tiles:
            def chunk(r0):
                s = s_ref[r0:r0 + ATTN_ROWS, cs]
                if masked:
                    key = lax.broadcasted_iota(jnp.int32, (ATTN_ROWS, MXU_N), 0) + r0
                    qry = lax.broadcasted_iota(jnp.int32, (ATTN_ROWS, MXU_N), 1) + cs.start % tq
                    s = jnp.where(key <= qry, s, -jnp.inf)
                return s

            m_blk = _fold_list([_fold_rows(chunk(r0), jnp.maximum) for r0 in chunks], jnp.maximum)
            m_prev = m_scr[:, cs]
            m_new = jnp.maximum(m_prev, jnp.max(m_blk, axis=0, keepdims=True))
            alpha = jnp.exp2(m_prev - m_new)
            ps, sums = [], []
            for r0 in chunks:
                p = jnp.exp2(chunk(r0) - m_new)
                sums.append(_fold_rows(p, jnp.add))
                ps.append(p.astype(BF16))
            l_blk = jnp.sum(_fold_list(sums, jnp.add), axis=0, keepdims=True)
            l_scr[:, cs] = alpha * l_scr[:, cs] + l_blk
            acc_scr[:, cs] = alpha * acc_scr[:, cs] + _dot(vtb, jnp.concatenate(ps, axis=0))
            m_scr[:, cs] = m_new

    scores(0, sa_scr)

    def body(i, carry):
        j = 2 * i
        scores(j + 1, sb_scr)
        softmax_pv(j, sa_scr, False)

        @pl.when(j + 1 < qi)
        def _():
            scores(j + 2, sa_scr)
            softmax_pv(j + 1, sb_scr, False)

        return carry

    lax.fori_loop(0, (qi + 1) // 2, body, 0)

    @pl.when(lax.rem(qi, 2) == 0)
    def _():
        softmax_pv(qi, sa_scr, True)

    @pl.when(lax.rem(qi, 2) == 1)
    def _():
        softmax_pv(qi, sb_scr, True)

    o = acc_scr[...] / l_scr[...]
    lam = _lam(lam_ref, lam_init)
    o = (o[:, 0:tq] - lam * o[:, tq:2 * tq]).T
    o_ref[...] = (_rms(o, sg_ref[...]) * (1.0 - lam_init)).astype(BF16)


def _attn_prompt(dqt, kb, vt, lam_p, sg, *, nb, t, lam_init):
    tq = 512
    assert t % tq == 0 and tq % MXU_N == 0
    nq = t // tq
    return pl.pallas_call(
        functools.partial(_attn_p_kernel, tq=tq, lam_init=lam_init),
        grid=(nb, DIFF_HEADS, nq),
        in_specs=[pl.BlockSpec((DIFF_HW, tq), lambda b, h, i: (h, b * nq + i)),
                  pl.BlockSpec((t, DIFF_HW), lambda b, h, i: (b, h)),
                  pl.BlockSpec((DIFF_HW, t), lambda b, h, i: (h, b)),
                  _resident(lam_p.shape), _resident(sg.shape)],
        out_specs=pl.BlockSpec((tq, DIFF_HW), lambda b, h, i: (b * nq + i, h)),
        out_shape=jax.ShapeDtypeStruct((nb * t, DIFF_WIDTH), BF16),
        scratch_shapes=[pltpu.VMEM((1, 2 * tq), F32),
                        pltpu.VMEM((1, 2 * tq), F32),
                        pltpu.VMEM((DIFF_HW, 2 * tq), F32),
                        pltpu.VMEM((DIFF_HW, 2 * tq), BF16),
                        pltpu.VMEM((tq, 2 * tq), F32),
                        pltpu.VMEM((tq, 2 * tq), F32)],
        compiler_params=_params(3),
        name="attn_prompt",
    )(dqt, kb, vt, lam_p, sg)


def _attn_s_kernel(pt_ref, q_ref, kn_ref, vn_ref, lam_ref, sg_ref, ck_hbm, cv_hbm, o_ref,
                   kbuf, vbuf, bias, sem, *, layer, n_pages, prow, ts, lam_init):
    b = pl.program_id(0)
    nb = pl.num_programs(0)
    slot = lax.rem(b, 2)
    past = n_pages * prow
    new = ts * DIFF_HEADS
    rows = kbuf.shape[1]
    nq = 2 * DIFF_HEADS * ts

    def page_copies(bb, sl):
        cps = []
        for p in range(n_pages):
            pg = pt_ref[bb, p]
            dst = pl.ds(p * prow, prow)
            cps.append(pltpu.make_async_copy(ck_hbm.at[layer, pg], kbuf.at[sl, dst], sem.at[0, sl]))
            cps.append(pltpu.make_async_copy(cv_hbm.at[layer, pg], vbuf.at[sl, dst], sem.at[1, sl]))
        return cps

    @pl.when(b == 0)
    def _():
        for sl in range(2):
            kbuf[sl, past:rows, :] = jnp.zeros((rows - past, DIFF_HW), F32)
            vbuf[sl, past:rows, :] = jnp.zeros((rows - past, DIFF_HW), F32)
        r = lax.broadcasted_iota(jnp.int32, (nq, rows), 0)
        c = lax.broadcasted_iota(jnp.int32, (nq, rows), 1)
        same_head = lax.rem(c, DIFF_HEADS) == r // (2 * ts)
        causal = c // DIFF_HEADS - past // DIFF_HEADS <= lax.rem(r, ts)
        bias[...] = jnp.where(jnp.logical_and(same_head, causal), 0.0, -jnp.inf)
        for cp in page_copies(0, 0):
            cp.start()

    @pl.when(b + 1 < nb)
    def _():
        for cp in page_copies(b + 1, 1 - slot):
            cp.start()

    for cp in page_copies(b, slot):
        cp.wait()

    kbuf[slot, past:past + new, :] = kn_ref[...]
    vbuf[slot, past:past + new, :] = vn_ref[...]

    q = q_ref[...].astype(F32)
    half = lax.broadcasted_iota(jnp.int32, (ts, DIFF_HW), 1) // DIFF_DH
    qrows = []
    for h in range(DIFF_HEADS):
        qh = q[:, h * DIFF_HW:(h + 1) * DIFF_HW]
        qrows += [jnp.where(half == 0, qh, 0.0), jnp.where(half == 1, qh, 0.0)]
    qrows = jnp.concatenate(qrows, axis=0).astype(BF16)
    s = _dot_nt(qrows, kbuf[slot].astype(BF16)) + bias[...]
    m = jnp.max(s, axis=-1, keepdims=True)
    p = jnp.exp(s - m)
    l = jnp.sum(p, axis=-1, keepdims=True)
    pv = _dot(p.astype(BF16), vbuf[slot].astype(BF16)) / l
    lam = _lam(lam_ref, lam_init)
    sg = sg_ref[...]
    outs = []
    for h in range(DIFF_HEADS):
        o = pv[2 * h * ts:(2 * h + 1) * ts] - lam * pv[(2 * h + 1) * ts:(2 * h + 2) * ts]
        outs.append(_rms(o, sg) * (1.0 - lam_init))
    o_ref[...] = jnp.concatenate(outs, axis=-1).astype(BF16)


def _attn_sample(page_table, dq, kn, vn, lam_p, sg, ck, cv, *, layer, nb, ts, lam_init):
    n_pages = page_table.shape[1]
    prow = ck.shape[2]
    new = ts * DIFF_HEADS
    rows = n_pages * prow + LANE
    nq = 2 * DIFF_HEADS * ts
    assert new <= LANE and ts % SUBLANE == 0
    const = lambda a: pl.BlockSpec(a.shape, lambda b, pt: (0,) * a.ndim, pipeline_mode=pl.Buffered(1))
    grid_spec = pltpu.PrefetchScalarGridSpec(
        num_scalar_prefetch=1,
        grid=(nb,),
        in_specs=[pl.BlockSpec((ts, DIFF_WIDTH), lambda b, pt: (b, 0)),
                  pl.BlockSpec((new, DIFF_HW), lambda b, pt: (b, 0)),
                  pl.BlockSpec((new, DIFF_HW), lambda b, pt: (b, 0)),
                  const(lam_p), const(sg),
                  pl.BlockSpec(memory_space=pl.ANY), pl.BlockSpec(memory_space=pl.ANY)],
        out_specs=pl.BlockSpec((ts, DIFF_WIDTH), lambda b, pt: (b, 0)),
        scratch_shapes=[pltpu.VMEM((2, rows, DIFF_HW), F32),
                        pltpu.VMEM((2, rows, DIFF_HW), F32),
                        pltpu.VMEM((nq, rows), F32),
                        pltpu.SemaphoreType.DMA((2, 2))],
    )
    return pl.pallas_call(
        functools.partial(_attn_s_kernel, layer=layer, n_pages=n_pages, prow=prow, ts=ts,
                          lam_init=lam_init),
        grid_spec=grid_spec,
        out_shape=jax.ShapeDtypeStruct((nb * ts, DIFF_WIDTH), BF16),
        compiler_params=_params(1),
        name="attn_sample",
    )(page_table, dq, kn, vn, lam_p, sg, ck, cv)


def _gla_kernel(q_ref, v_ref, la_ref, gg_ref, kt_ref, lat_ref, s0_ref, gn_ref,
                o_ref, sout_ref, s_scr, *, c, groups):
    step = pl.program_id(1)
    r = c // groups

    hr = lax.broadcasted_iota(jnp.int32, (GLA_WIDTH, GLA_WIDTH), 0) // GLA_DH
    hc = lax.broadcasted_iota(jnp.int32, (GLA_WIDTH, GLA_WIDTH), 1) // GLA_DH
    same_head = hr == hc

    @pl.when(step == 0)
    def _():
        er = lax.broadcasted_iota(jnp.int32, (GLA_DH, GLA_WIDTH), 0)
        ec = lax.broadcasted_iota(jnp.int32, (GLA_DH, GLA_WIDTH), 1)
        expand = jnp.where(lax.rem(ec, GLA_DH) == er, 1.0, 0.0).astype(BF16)
        for gi in range(groups):
            s_scr[gi] = jnp.where(same_head, _dot_split(s0_ref[gi], expand), 0.0)

    row = lax.broadcasted_iota(jnp.int32, (c, c), 0)
    col = lax.broadcasted_iota(jnp.int32, (c, c), 1)
    if groups > 1:
        same = (row // r) == (col // r)
        causal = jnp.logical_and(col <= row, same)
        causal_t = jnp.logical_and(row <= col, same)
    else:
        causal = col <= row
        causal_t = row <= col
    ltri = jnp.where(causal, 1.0, 0.0).astype(BF16)
    utri = jnp.where(causal_t, 1.0, 0.0).astype(BF16)

    cum = sum(_dot(ltri, part) for part in _split_bf16(la_ref[...], 3))
    cum_t = _dot_split(lat_ref[...], utri)

    q = q_ref[...]
    kt = kt_ref[...]
    v = v_ref[...]
    vb = v.astype(BF16)
    qd = q * jnp.exp(cum)
    if groups == 1:
        qt = q * jnp.exp(cum - cum[c // 2:c // 2 + 1, :])
        ktt = kt * jnp.exp(cum_t[:, c // 2:c // 2 + 1] - cum_t)
    else:
        qt = qd
        ktt = kt * jnp.exp(-cum_t)
    kttb = ktt.astype(BF16)

    head_l = lax.broadcasted_iota(jnp.int32, (c, GLA_WIDTH), 1) // GLA_DH
    o = jnp.zeros((c, GLA_WIDTH), F32)
    for h in range(GLA_HEADS):
        mh = head_l == h
        att = _dot(jnp.where(mh, qt, 0.0).astype(BF16), kttb)
        att = jnp.where(causal, att, 0.0)
        o = o + _dot(att.astype(BF16), jnp.where(mh, v, 0.0).astype(BF16))

    colg = lax.broadcasted_iota(jnp.int32, (GLA_WIDTH, c), 1) // r
    inter = []
    for gi in range(groups):
        s_old = s_scr[gi]
        inter.append(_dot(qd[gi * r:(gi + 1) * r, :].astype(BF16), s_old.astype(BF16)))
        last_t = cum_t[:, (gi + 1) * r - 1:(gi + 1) * r]
        if groups > 1:
            ing = colg == gi
            kd_t = jnp.where(ing, kt * jnp.exp(jnp.where(ing, last_t - cum_t, 0.0)), 0.0)
        else:
            kd_t = kt * jnp.exp(last_t - cum_t)
        upd = _dot(kd_t.astype(BF16), vb)
        s_scr[gi] = s_old * jnp.exp(last_t) + jnp.where(same_head, upd, 0.0)
    o = o + (inter[0] if groups == 1 else jnp.concatenate(inter, axis=0))

    avg = jnp.where(same_head, 1.0 / GLA_DH, 0.0).astype(BF16)
    msq = _dot_split(o * o, avg, parts=2)
    gg = gg_ref[...]
    o_ref[...] = (o * lax.rsqrt(msq + EPS) * gn_ref[...] * (gg * jax.nn.sigmoid(gg))).astype(BF16)

    @pl.when(step == pl.num_programs(1) - 1)
    def _():
        cr = lax.broadcasted_iota(jnp.int32, (GLA_WIDTH, GLA_DH), 0)
        cc = lax.broadcasted_iota(jnp.int32, (GLA_WIDTH, GLA_DH), 1)
        compact = jnp.where(lax.rem(cr, GLA_DH) == cc, 1.0, 0.0).astype(BF16)
        for gi in range(groups):
            sout_ref[gi] = _dot_split(s_scr[gi], compact)


def _gla(gq, gv, la, gg, kt, lat, s0, gn, *, nseq, t):
    c = LANE
    if t >= c:
        assert t % c == 0
        groups, nb, nsteps = 1, nseq, t // c
    else:
        assert c % t == 0 and nseq % (c // t) == 0
        groups, nb, nsteps = c // t, nseq // (c // t), 1
    m = nseq * t
    row = pl.BlockSpec((c, GLA_WIDTH), lambda b, s: (b * nsteps + s, 0))
    colb = pl.BlockSpec((GLA_WIDTH, c), lambda b, s: (0, b * nsteps + s))
    st = pl.BlockSpec((groups, GLA_WIDTH, GLA_DH), lambda b, s: (b, 0, 0))
    return pl.pallas_call(
        functools.partial(_gla_kernel, c=c, groups=groups),
        grid=(nb, nsteps),
        in_specs=[row, row, row, row, colb, colb, st,
                  pl.BlockSpec(gn.shape, lambda b, s: (0, 0))],
        out_specs=[row, st],
        out_shape=[jax.ShapeDtypeStruct((m, GLA_WIDTH), BF16),
                   jax.ShapeDtypeStruct((nseq, GLA_WIDTH, GLA_DH), F32)],
        scratch_shapes=[pltpu.VMEM((groups, GLA_WIDTH, GLA_WIDTH), F32)],
        compiler_params=_params(2),
        name="gla",
    )(gq, gv, la, gg, kt, lat, s0, gn)


def _layer_weights(l, norm_gain, ffn_w_in, ffn_w_out, w_in, w_out, conv_w, conv_b, conv_ln_g,
                   conv_ln_b, conv_pw, diff_lam, diff_subln_g, gla_w_gate, gla_b_gate, gla_norm_g):
    pad_r = LANE - GLA_RANK
    wga = jnp.pad(w_in[l][:, OFF_GA:], ((0, 0), (0, pad_r))).astype(BF16)
    wgate = jnp.pad(gla_w_gate[l], ((0, pad_r), (0, 0))).astype(BF16)
    win = w_in[l][:, :OFF_GA].astype(BF16)
    dense_in = [norm_gain[l], ffn_w_in[l, 0].astype(BF16), ffn_w_out[l, 0].astype(BF16), win,
                win[:, OFF_GK:OFF_GV].T, wga, wga.T, wgate, wgate.T,
                gla_b_gate[l].reshape(1, GLA_WIDTH), gla_b_gate[l].reshape(GLA_WIDTH, 1)]
    return dict(
        dense_in_s=dense_in,
        dense_in_p=dense_in + [win[:, OFF_DQ:OFF_DK].T, win[:, OFF_DV:OFF_GQ].T],
        dense_out=[norm_gain[l], w_out[l].astype(BF16), ffn_w_in[l, 1].astype(BF16),
                   ffn_w_out[l, 1].astype(BF16)],
        conv=[conv_w[l], conv_b[l].reshape(1, CONV_CH), conv_ln_g[l].reshape(1, CONV_CH),
              conv_ln_b[l].reshape(1, CONV_CH), conv_pw[l].astype(BF16)],
        lam=diff_lam[l], sg=diff_subln_g[l].reshape(1, DIFF_HW),
        gn=jnp.tile(gla_norm_g[l], GLA_HEADS).reshape(1, GLA_WIDTH),
    )


def kernel(x_prompt, x_sample, cache_k, cache_v, page_table, state_conv, state_gla, norm_gain,
           ffn_w_in, ffn_w_out, w_in, w_out, conv_w, conv_b, conv_ln_g, conv_ln_b, conv_pw,
           diff_lam, diff_subln_g, gla_w_gate, gla_b_gate, gla_norm_g):
    bp, tp, d = x_prompt.shape
    bs, ts, _ = x_sample.shape
    depth = norm_gain.shape[0]
    n_pool, page = cache_k.shape[1], cache_k.shape[2]
    ck = cache_k.reshape(depth, n_pool, page * DIFF_HEADS, DIFF_HW)
    cv = cache_v.reshape(depth, n_pool, page * DIFF_HEADS, DIFF_HW)

    xp = x_prompt.reshape(bp * tp, d)
    xs = x_sample.reshape(bs * ts, d)
    conv0 = jnp.zeros((bp, CONV_WIDTH - 1, CONV_CH), F32)
    s0p = jnp.zeros((bp, GLA_WIDTH, GLA_DH), F32)
    outs = [[] for _ in range(8)]
    for l in range(depth):
        w = _layer_weights(l, norm_gain, ffn_w_in, ffn_w_out, w_in, w_out, conv_w, conv_b,
                           conv_ln_g, conv_ln_b, conv_pw, diff_lam, diff_subln_g, gla_w_gate,
                           gla_b_gate, gla_norm_g)
        lam_init = 0.8 - 0.6 * math.exp(-0.3 * l)

        x1, u, kn, vn, gq, gv, la, gg, kt, lat, dqt, kb, vt = _dense_in(
            xp, w["dense_in_p"], tm=256, prompt=True)
        co, ncp = _conv(u, conv0, *w["conv"], nb=bp, t=tp)
        do = _attn_prompt(dqt, kb, vt, w["lam"], w["sg"], nb=bp, t=tp, lam_init=lam_init)
        go, sp = _gla(gq, gv, la, gg, kt, lat, s0p, w["gn"], nseq=bp, t=tp)
        xp = _dense_out(x1, co, do, go, *w["dense_out"], tm=512)
        outs[0].append(kn.reshape(bp, tp, DIFF_HEADS, DIFF_HW))
        outs[1].append(vn.reshape(bp, tp, DIFF_HEADS, DIFF_HW))
        outs[2].append(ncp)
        outs[3].append(sp.reshape(bp, GLA_HEADS, GLA_DH, GLA_DH))

        x1, u, kn, vn, gq, gv, la, gg, kt, lat, dq = _dense_in(
            xs, w["dense_in_s"], tm=256, prompt=False)
        co, ncs = _conv(u, state_conv[l], *w["conv"], nb=bs, t=ts)
        kn4 = kn.reshape(bs, ts, DIFF_HEADS, DIFF_HW)
        vn4 = vn.reshape(bs, ts, DIFF_HEADS, DIFF_HW)
        do = _attn_sample(page_table, dq, kn4.reshape(-1, DIFF_HW), vn4.reshape(-1, DIFF_HW),
                          w["lam"], w["sg"], ck, cv, layer=l, nb=bs, ts=ts, lam_init=lam_init)
        go, ss = _gla(gq, gv, la, gg, kt, lat, state_gla[l].reshape(bs, GLA_WIDTH, GLA_DH),
                      w["gn"], nseq=bs, t=ts)
        xs = _dense_out(x1, co, do, go, *w["dense_out"], tm=512)
        outs[4].append(kn4)
        outs[5].append(vn4)
        outs[6].append(ncs)
        outs[7].append(ss.reshape(bs, GLA_HEADS, GLA_DH, GLA_DH))

    return (xp.reshape(bp, tp, d), xs.reshape(bs, ts, d)) + tuple(jnp.stack(o) for o in outs)
```

```python
import functools
import math

import jax
import jax.numpy as jnp
from jax import lax
from jax.experimental import pallas as pl
from jax.experimental.pallas import tpu as pltpu

F32 = jnp.float32
BF16 = jnp.bfloat16
EPS = 1e-6
LOG2E = math.log2(math.e)

CONV_CH = 256
CONV_WIDTH = 31
CONV_HALO = 32
DIFF_HEADS = 4
DIFF_DH = 64
DIFF_HW = 2 * DIFF_DH
DIFF_WIDTH = DIFF_HEADS * DIFF_HW
GLA_HEADS = 4
GLA_DH = 64
GLA_WIDTH = GLA_HEADS * GLA_DH
GLA_RANK = 16
GLA_TAU = 16.0
OFF_DQ = 2 * CONV_CH
OFF_DK = OFF_DQ + DIFF_WIDTH
OFF_DV = OFF_DK + DIFF_WIDTH
OFF_GQ = OFF_DV + DIFF_WIDTH
OFF_GK = OFF_GQ + GLA_WIDTH
OFF_GV = OFF_GK + GLA_WIDTH
OFF_GG = OFF_GV + GLA_WIDTH
OFF_GA = OFF_GG + GLA_WIDTH

LANE = 128
SUBLANE = 8
MXU_N = 256
ATTN_ROWS = 64
VMEM_LIMIT = 56 * 1024 * 1024

NT_DIMS = (((1,), (1,)), ((), ()))


def _dot(a, b):
    return jnp.dot(a, b, preferred_element_type=F32)


def _dot_nt(a, b):
    return lax.dot_general(a, b, NT_DIMS, preferred_element_type=F32)


def _rms(x, g):
    return x * lax.rsqrt(jnp.mean(x * x, axis=-1, keepdims=True) + EPS) * g


def _log_sigmoid(x):
    return jnp.minimum(x, 0.0) - jnp.log1p(jnp.exp(-jnp.abs(x)))


def _split_bf16(x, parts):
    out = []
    r = x
    for _ in range(parts - 1):
        p = r.astype(BF16)
        out.append(p)
        r = r - p.astype(F32)
    out.append(r.astype(BF16))
    return out


def _dot_split(x, w01, parts=3):
    return sum(_dot(p, w01) for p in _split_bf16(x, parts))


def _params(n_axes):
    return pltpu.CompilerParams(dimension_semantics=("arbitrary",) * n_axes,
                                vmem_limit_bytes=VMEM_LIMIT)


def _resident(shape):
    nd = len(shape)
    return pl.BlockSpec(shape, lambda *_: (0,) * nd, pipeline_mode=pl.Buffered(1))


def _ffn(x, g_pre, g_post, wgu_ref, wo_ref, a_scr, d_ff, chunk):
    h = _rms(x, g_pre).astype(BF16)
    for c in range(0, d_ff, chunk):
        gate = _dot(h, wgu_ref[:, c:c + chunk])
        up = _dot(h, wgu_ref[:, d_ff + c:d_ff + c + chunk])
        a_scr[:, c:c + chunk] = (gate * jax.nn.sigmoid(gate) * up).astype(BF16)
    y = _dot(a_scr[...], wo_ref[...])
    return x + 0.5 * _rms(y, g_post)


def _dense_in_kernel(*refs, d_ff, chunk, prompt):
    if prompt:
        (x_ref, g_ref, wgu_ref, wo_ref, win_ref, wkt_ref, wga_ref, wgat_ref, wgate_ref, wgatet_ref,
         bg_ref, bgt_ref, wdqt_ref, wvt_ref, _, _,
         x1_ref, u_ref, kn_ref, vn_ref, gq_ref, gv_ref, la_ref, gg_ref, kt_ref, lat_ref,
         dqt_ref, kb_ref, vt_ref, a_scr) = refs
    else:
        (x_ref, g_ref, wgu_ref, wo_ref, win_ref, wkt_ref, wga_ref, wgat_ref, wgate_ref, wgatet_ref,
         bg_ref, bgt_ref, _, _,
         x1_ref, u_ref, kn_ref, vn_ref, gq_ref, gv_ref, la_ref, gg_ref, kt_ref, lat_ref,
         dq_ref, a_scr) = refs
    g = g_ref[...]
    x1 = _ffn(x_ref[...], g[0:1], g[1:2], wgu_ref, wo_ref, a_scr, d_ff, chunk)
    x1_ref[...] = x1
    h = _rms(x1, g[2:3]).astype(BF16)

    ca = _dot(h, win_ref[:, 0:CONV_CH])
    cg = _dot(h, win_ref[:, CONV_CH:OFF_DQ])
    u_ref[...] = ca * jax.nn.sigmoid(cg)

    k = _dot(h, win_ref[:, OFF_DK:OFF_DV])
    v = _dot(h, win_ref[:, OFF_DV:OFF_GQ])
    tm = k.shape[0]
    for hd in range(DIFF_HEADS):
        lanes = slice(hd * DIFF_HW, (hd + 1) * DIFF_HW)
        kn_ref[pl.ds(hd, tm, stride=DIFF_HEADS), :] = k[:, lanes]
        vn_ref[pl.ds(hd, tm, stride=DIFF_HEADS), :] = v[:, lanes]
    if prompt:
        dqt_ref[...] = (_dot_nt(wdqt_ref[...], h) * (DIFF_DH ** -0.5 * LOG2E)).astype(BF16)
        kb_ref[...] = k.astype(BF16)
        vt_ref[...] = _dot_nt(wvt_ref[...], h).astype(BF16)
    else:
        dq_ref[...] = (_dot(h, win_ref[:, OFF_DQ:OFF_DK]) * (DIFF_DH ** -0.5)).astype(BF16)

    gq_ref[...] = _dot(h, win_ref[:, OFF_GQ:OFF_GK]) * (GLA_DH ** -0.5)
    gv_ref[...] = _dot(h, win_ref[:, OFF_GV:OFF_GG])
    gg_ref[...] = _dot(h, win_ref[:, OFF_GG:OFF_GA])
    ga = _dot(h, wga_ref[...])
    a_logit = _dot(ga.astype(BF16), wgate_ref[...]) + bg_ref[...]
    la_ref[...] = _log_sigmoid(a_logit) * (1.0 / GLA_TAU)

    kt_ref[...] = _dot_nt(wkt_ref[...], h)
    gat = _dot_nt(wgat_ref[...], h)
    a_logit_t = _dot(wgatet_ref[...], gat.astype(BF16)) + bgt_ref[...]
    lat_ref[...] = _log_sigmoid(a_logit_t) * (1.0 / GLA_TAU)


def _dense_in(x, weights, k_all, v_all, *, layer, tm, prompt):
    m, d = x.shape
    d_ff = weights[2].shape[0]
    chunk = MXU_N
    assert m % tm == 0 and d_ff % chunk == 0
    row = lambda w: pl.BlockSpec((tm, w), lambda i: (i, 0))
    col = lambda h: pl.BlockSpec((h, tm), lambda i: (0, i))
    rows = lambda w, dt: (jax.ShapeDtypeStruct((m, w), dt), row(w))
    cols = lambda h, dt: (jax.ShapeDtypeStruct((h, m), dt), col(h))
    kv = lambda a: (jax.ShapeDtypeStruct(a.shape, a.dtype),
                    pl.BlockSpec((None, tm * DIFF_HEADS, DIFF_HW), lambda i: (layer, i, 0)))
    outs = [rows(d, F32),
            rows(CONV_CH, F32),
            kv(k_all), kv(v_all),
            rows(GLA_WIDTH, F32), rows(GLA_WIDTH, F32),
            rows(GLA_WIDTH, F32), rows(GLA_WIDTH, F32),
            cols(GLA_WIDTH, F32), cols(GLA_WIDTH, F32)]
    if prompt:
        outs += [cols(DIFF_WIDTH, BF16), rows(DIFF_WIDTH, BF16), cols(DIFF_WIDTH, BF16)]
    else:
        outs += [rows(DIFF_WIDTH, BF16)]
    return pl.pallas_call(
        functools.partial(_dense_in_kernel, d_ff=d_ff, chunk=chunk, prompt=prompt),
        grid=(m // tm,),
        in_specs=[row(d)] + [_resident(w.shape) for w in weights]
        + [pl.BlockSpec(memory_space=pl.ANY)] * 2,
        out_specs=[s for _, s in outs],
        out_shape=[s for s, _ in outs],
        input_output_aliases={1 + len(weights): 2, 2 + len(weights): 3},
        scratch_shapes=[pltpu.VMEM((tm, d_ff), BF16)],
        compiler_params=_params(1),
        name="dense_in",
    )(x, *weights, k_all, v_all)


def _dense_out_kernel(x_ref, co_ref, do_ref, go_ref, g_ref, wout_ref, wgu_ref, wo_ref,
                      x2_ref, a_scr, *, d_ff, chunk):
    g = g_ref[...]
    m = (_dot(co_ref[...], wout_ref[0:CONV_CH, :])
         + _dot(do_ref[...], wout_ref[CONV_CH:CONV_CH + DIFF_WIDTH, :])
         + _dot(go_ref[...], wout_ref[CONV_CH + DIFF_WIDTH:, :]))
    x = x_ref[...] + _rms(m, g[3:4])
    x2_ref[...] = _ffn(x, g[4:5], g[5:6], wgu_ref, wo_ref, a_scr, d_ff, chunk)


def _dense_out(x, co, do, go, g, wout, wgu, wo, *, tm):
    m, d = x.shape
    d_ff = wo.shape[0]
    chunk = MXU_N
    assert m % tm == 0 and d_ff % chunk == 0
    row = lambda w: pl.BlockSpec((tm, w), lambda i: (i, 0))
    weights = (g, wout, wgu, wo)
    return pl.pallas_call(
        functools.partial(_dense_out_kernel, d_ff=d_ff, chunk=chunk),
        grid=(m // tm,),
        in_specs=[row(d), row(CONV_CH), row(DIFF_WIDTH), row(GLA_WIDTH)]
        + [_resident(w.shape) for w in weights],
        out_specs=row(d),
        out_shape=jax.ShapeDtypeStruct((m, d), F32),
        scratch_shapes=[pltpu.VMEM((tm, d_ff), BF16)],
        compiler_params=_params(1),
        name="dense_out",
    )(x, co, do, go, *weights)


def _conv_kernel(u_ref, buf_ref, w_ref, b_ref, lg_ref, lb_ref, pw_ref, o_ref, nc_ref, scr,
                 *, nseq, t, tt):
    hist = CONV_WIDTH - 1
    lead = CONV_HALO - hist
    w = w_ref[...]
    bias = b_ref[...]
    lg = lg_ref[...]
    lb = lb_ref[...]
    pw = pw_ref[...]
    scr[0:lead, :] = jnp.zeros((lead, CONV_CH), F32)

    for s in range(nseq):
        scr[lead:CONV_HALO, :] = buf_ref[s]
        scr[CONV_HALO:CONV_HALO + t, :] = u_ref[s]
        nc_ref[s] = scr[t + lead:t + CONV_HALO, :]

        def tile(i, carry):
            t0 = pl.multiple_of(i * tt, tt)
            win = scr[pl.ds(t0, tt + CONV_HALO), :]
            acc = jnp.zeros((tt, CONV_CH), F32) + bias
            for r in range(SUBLANE):
                taps = [j for j in range(CONV_WIDTH) if (lead + j) % SUBLANE == r]
                sh = win if r == 0 else win[r:r + tt + CONV_HALO - SUBLANE, :]
                for j in taps:
                    a = lead + j - r
                    acc = acc + w[j:j + 1, :] * sh[a:a + tt, :]
            mu = jnp.mean(acc, axis=-1, keepdims=True)
            xc = acc - mu
            y = xc * lax.rsqrt(jnp.mean(xc * xc, axis=-1, keepdims=True) + EPS) * lg + lb
            y = y * jax.nn.sigmoid(y)
            o_ref[s, pl.ds(t0, tt), :] = _dot(y.astype(BF16), pw).astype(BF16)
            return carry

        lax.fori_loop(0, t // tt, tile, 0)


def _conv(u, buf, w, b, lg, lb, pw, *, nb, t):
    tt = min(t, 256)
    nseq = 1 if t >= 256 else min(nb, 16)
    assert t % tt == 0 and tt % SUBLANE == 0 and nb % nseq == 0
    hist = CONV_WIDTH - 1
    u3 = u.reshape(nb, t, CONV_CH)
    small = lambda a: _resident(a.shape)
    out, new_conv = pl.pallas_call(
        functools.partial(_conv_kernel, nseq=nseq, t=t, tt=tt),
        grid=(nb // nseq,),
        in_specs=[pl.BlockSpec((nseq, t, CONV_CH), lambda i: (i, 0, 0)),
                  pl.BlockSpec((nseq, hist, CONV_CH), lambda i: (i, 0, 0)),
                  small(w), small(b), small(lg), small(lb), small(pw)],
        out_specs=[pl.BlockSpec((nseq, t, CONV_CH), lambda i: (i, 0, 0)),
                   pl.BlockSpec((nseq, hist, CONV_CH), lambda i: (i, 0, 0))],
        out_shape=[jax.ShapeDtypeStruct((nb, t, CONV_CH), BF16),
                   jax.ShapeDtypeStruct((nb, hist, CONV_CH), F32)],
        scratch_shapes=[pltpu.VMEM((CONV_HALO + t, CONV_CH), F32)],
        compiler_params=_params(1),
        name="conv",
    )(u3, buf, w, b, lg, lb, pw)
    return out.reshape(nb * t, CONV_CH), new_conv


def _fold_rows(x, op):
    n = x.shape[0]
    while n > SUBLANE:
        n //= 2
        x = op(x[:n], x[n:2 * n])
    return x


def _fold_list(xs, op):
    while len(xs) > 1:
        xs = [op(xs[i], xs[i + 1]) for i in range(0, len(xs) - 1, 2)] + (xs[-1:] if len(xs) % 2 else [])
    return xs[0]


def _lam(lam_ref, lam_init):
    dl = lam_ref[...]
    s1 = jnp.sum(dl[0:1] * dl[1:2], axis=-1, keepdims=True)
    s2 = jnp.sum(dl[2:3] * dl[3:4], axis=-1, keepdims=True)
    return jnp.exp(s1) - jnp.exp(s2) + lam_init


def _attn_p_kernel(qt_ref, k_ref, vt_ref, lam_ref, sg_ref, o_ref, m_scr, l_scr, acc_scr,
                   qs_scr, sa_scr, sb_scr, *, tq, tk, lam_init):
    qi = pl.program_id(2)
    qt = qt_ref[...].astype(F32)
    sub = lax.broadcasted_iota(jnp.int32, (DIFF_HW, tq), 0)
    qs_scr[...] = jnp.concatenate([jnp.where(sub < DIFF_DH, qt, 0.0),
                                   jnp.where(sub >= DIFF_DH, qt, 0.0)], axis=1).astype(BF16)
    m_scr[...] = jnp.full(m_scr.shape, -jnp.inf, F32)
    l_scr[...] = jnp.zeros(l_scr.shape, F32)
    acc_scr[...] = jnp.zeros(acc_scr.shape, F32)

    tiles = [slice(c * MXU_N, (c + 1) * MXU_N) for c in range(2 * tq // MXU_N)]

    def live_tiles(diag):
        if diag is None:
            return tiles
        return [cs for cs in tiles if cs.start % tq + MXU_N - 1 >= diag * tk]

    def scores(j, s_ref, diag=None):
        k0 = pl.multiple_of(j * tk, tk)
        kb = k_ref[pl.ds(k0, tk), :]
        for cs in live_tiles(diag):
            s_ref[:, cs] = _dot(kb, qs_scr[:, cs])

    def softmax_pv(j, s_ref, diag=None):
        k0 = pl.multiple_of(j * tk, tk)
        vtb = vt_ref[:, pl.ds(k0, tk)]
        chunks = range(0, tk, ATTN_ROWS)
        for cs in live_tiles(diag):
            q0 = cs.start % tq
            masked = diag is not None and q0 < (diag + 1) * tk - 1

            def chunk(r0):
                s = s_ref[r0:r0 + ATTN_ROWS, cs]
                if masked:
                    key = lax.broadcasted_iota(jnp.int32, (ATTN_ROWS, MXU_N), 0) + (diag * tk + r0)
                    qry = lax.broadcasted_iota(jnp.int32, (ATTN_ROWS, MXU_N), 1) + q0
                    s = jnp.where(key <= qry, s, -jnp.inf)
                return s

            m_blk = _fold_list([_fold_rows(chunk(r0), jnp.maximum) for r0 in chunks], jnp.maximum)
            m_prev = m_scr[:, cs]
            m_new = jnp.maximum(m_prev, jnp.max(m_blk, axis=0, keepdims=True))
            alpha = jnp.exp2(m_prev - m_new)
            ps, sums = [], []
            for r0 in chunks:
                p = jnp.exp2(chunk(r0) - m_new)
                sums.append(_fold_rows(p, jnp.add))
                ps.append(p.astype(BF16))
            l_blk = jnp.sum(_fold_list(sums, jnp.add), axis=0, keepdims=True)
            l_scr[:, cs] = alpha * l_scr[:, cs] + l_blk
            acc_scr[:, cs] = alpha * acc_scr[:, cs] + _dot(vtb, jnp.concatenate(ps, axis=0))
            m_scr[:, cs] = m_new

    scores(0, sa_scr)

    def body(i, carry):
        j = 2 * i
        scores(j + 1, sb_scr)
        softmax_pv(j, sa_scr)
        scores(j + 2, sa_scr)
        softmax_pv(j + 1, sb_scr)
        return carry

    lax.fori_loop(0, qi, body, 0)
    scores(2 * qi + 1, sb_scr, diag=1)
    softmax_pv(2 * qi, sa_scr, diag=0)
    softmax_pv(2 * qi + 1, sb_scr, diag=1)

    o = acc_scr[...] / l_scr[...]
    lam = _lam(lam_ref, lam_init)
    o = (o[:, 0:tq] - lam * o[:, tq:2 * tq]).T
    o_ref[...] = (_rms(o, sg_ref[...]) * (1.0 - lam_init)).astype(BF16)


def _attn_prompt(dqt, kb, vt, lam_p, sg, *, nb, t, lam_init):
    tq = 1024
    tk = tq // 2
    assert t % tq == 0 and tk % MXU_N == 0 and tk % ATTN_ROWS == 0
    nq = t // tq
    return pl.pallas_call(
        functools.partial(_attn_p_kernel, tq=tq, tk=tk, lam_init=lam_init),
        grid=(nb, DIFF_HEADS, nq),
        in_specs=[pl.BlockSpec((DIFF_HW, tq), lambda b, h, i: (h, b * nq + i)),
                  pl.BlockSpec((t, DIFF_HW), lambda b, h, i: (b, h)),
                  pl.BlockSpec((DIFF_HW, t), lambda b, h, i: (h, b)),
                  _resident(lam_p.shape), _resident(sg.shape)],
        out_specs=pl.BlockSpec((tq, DIFF_HW), lambda b, h, i: (b * nq + i, h)),
        out_shape=jax.ShapeDtypeStruct((nb * t, DIFF_WIDTH), BF16),
        scratch_shapes=[pltpu.VMEM((1, 2 * tq), F32),
                        pltpu.VMEM((1, 2 * tq), F32),
                        pltpu.VMEM((DIFF_HW, 2 * tq), F32),
                        pltpu.VMEM((DIFF_HW, 2 * tq), BF16),
                        pltpu.VMEM((tk, 2 * tq), F32),
                        pltpu.VMEM((tk, 2 * tq), F32)],
        compiler_params=_params(3),
        name="attn_prompt",
    )(dqt, kb, vt, lam_p, sg)


def _attn_s_kernel(pt_ref, q_ref, kn_ref, vn_ref, lam_ref, sg_ref, ck_hbm, cv_hbm, o_ref,
                   kbuf, vbuf, bias, sem, *, layer, n_pages, prow, ts, lam_init):
    b = pl.program_id(0)
    nb = pl.num_programs(0)
    slot = lax.rem(b, 2)
    past = n_pages * prow
    new = ts * DIFF_HEADS
    rows = kbuf.shape[1]
    nq = 2 * DIFF_HEADS * ts

    def page_copies(bb, sl):
        cps = []
        for p in range(n_pages):
            pg = pt_ref[bb, p]
            dst = pl.ds(p * prow, prow)
            cps.append(pltpu.make_async_copy(ck_hbm.at[layer, pg], kbuf.at[sl, dst], sem.at[0, sl]))
            cps.append(pltpu.make_async_copy(cv_hbm.at[layer, pg], vbuf.at[sl, dst], sem.at[1, sl]))
        return cps

    @pl.when(b == 0)
    def _():
        for sl in range(2):
            kbuf[sl, past:rows, :] = jnp.zeros((rows - past, DIFF_HW), F32)
            vbuf[sl, past:rows, :] = jnp.zeros((rows - past, DIFF_HW), F32)
        r = lax.broadcasted_iota(jnp.int32, (nq, rows), 0)
        c = lax.broadcasted_iota(jnp.int32, (nq, rows), 1)
        same_head = lax.rem(c, DIFF_HEADS) == r // (2 * ts)
        causal = c // DIFF_HEADS - past // DIFF_HEADS <= lax.rem(r, ts)
        bias[...] = jnp.where(jnp.logical_and(same_head, causal), 0.0, -jnp.inf)
        for cp in page_copies(0, 0):
            cp.start()

    @pl.when(b + 1 < nb)
    def _():
        for cp in page_copies(b + 1, 1 - slot):
            cp.start()

    for cp in page_copies(b, slot):
        cp.wait()

    kbuf[slot, past:past + new, :] = kn_ref[...]
    vbuf[slot, past:past + new, :] = vn_ref[...]

    q = q_ref[...].astype(F32)
    half = lax.broadcasted_iota(jnp.int32, (ts, DIFF_HW), 1) // DIFF_DH
    qrows = []
    for h in range(DIFF_HEADS):
        qh = q[:, h * DIFF_HW:(h + 1) * DIFF_HW]
        qrows += [jnp.where(half == 0, qh, 0.0), jnp.where(half == 1, qh, 0.0)]
    qrows = jnp.concatenate(qrows, axis=0).astype(BF16)
    s = _dot_nt(qrows, kbuf[slot].astype(BF16)) + bias[...]
    m = jnp.max(s, axis=-1, keepdims=True)
    p = jnp.exp(s - m)
    l = jnp.sum(p, axis=-1, keepdims=True)
    pv = _dot(p.astype(BF16), vbuf[slot].astype(BF16)) / l
    lam = _lam(lam_ref, lam_init)
    sg = sg_ref[...]
    outs = []
    for h in range(DIFF_HEADS):
        o = pv[2 * h * ts:(2 * h + 1) * ts] - lam * pv[(2 * h + 1) * ts:(2 * h + 2) * ts]
        outs.append(_rms(o, sg) * (1.0 - lam_init))
    o_ref[...] = jnp.concatenate(outs, axis=-1).astype(BF16)


def _attn_sample(page_table, dq, kn, vn, lam_p, sg, ck, cv, *, layer, nb, ts, lam_init):
    n_pages = page_table.shape[1]
    prow = ck.shape[2]
    new = ts * DIFF_HEADS
    rows = n_pages * prow + LANE
    nq = 2 * DIFF_HEADS * ts
    assert new <= LANE and ts % SUBLANE == 0
    const = lambda a: pl.BlockSpec(a.shape, lambda b, pt: (0,) * a.ndim, pipeline_mode=pl.Buffered(1))
    grid_spec = pltpu.PrefetchScalarGridSpec(
        num_scalar_prefetch=1,
        grid=(nb,),
        in_specs=[pl.BlockSpec((ts, DIFF_WIDTH), lambda b, pt: (b, 0)),
                  pl.BlockSpec((None, new, DIFF_HW), lambda b, pt: (layer, b, 0)),
                  pl.BlockSpec((None, new, DIFF_HW), lambda b, pt: (layer, b, 0)),
                  const(lam_p), const(sg),
                  pl.BlockSpec(memory_space=pl.ANY), pl.BlockSpec(memory_space=pl.ANY)],
        out_specs=pl.BlockSpec((ts, DIFF_WIDTH), lambda b, pt: (b, 0)),
        scratch_shapes=[pltpu.VMEM((2, rows, DIFF_HW), F32),
                        pltpu.VMEM((2, rows, DIFF_HW), F32),
                        pltpu.VMEM((nq, rows), F32),
                        pltpu.SemaphoreType.DMA((2, 2))],
    )
    return pl.pallas_call(
        functools.partial(_attn_s_kernel, layer=layer, n_pages=n_pages, prow=prow, ts=ts,
                          lam_init=lam_init),
        grid_spec=grid_spec,
        out_shape=jax.ShapeDtypeStruct((nb * ts, DIFF_WIDTH), BF16),
        compiler_params=_params(1),
        name="attn_sample",
    )(page_table, dq, kn, vn, lam_p, sg, ck, cv)


def _gla_kernel(q_ref, v_ref, la_ref, gg_ref, kt_ref, lat_ref, s0_ref, gn_ref,
                o_ref, sout_ref, s_scr, *, c, groups):
    step = pl.program_id(1)
    r = c // groups

    hr = lax.broadcasted_iota(jnp.int32, (GLA_WIDTH, GLA_WIDTH), 0) // GLA_DH
    hc = lax.broadcasted_iota(jnp.int32, (GLA_WIDTH, GLA_WIDTH), 1) // GLA_DH
    same_head = hr == hc

    @pl.when(step == 0)
    def _():
        er = lax.broadcasted_iota(jnp.int32, (GLA_DH, GLA_WIDTH), 0)
        ec = lax.broadcasted_iota(jnp.int32, (GLA_DH, GLA_WIDTH), 1)
        expand = jnp.where(lax.rem(ec, GLA_DH) == er, 1.0, 0.0).astype(BF16)
        for gi in range(groups):
            s_scr[gi] = jnp.where(same_head, _dot_split(s0_ref[gi], expand), 0.0)

    row = lax.broadcasted_iota(jnp.int32, (c, c), 0)
    col = lax.broadcasted_iota(jnp.int32, (c, c), 1)
    if groups > 1:
        same = (row // r) == (col // r)
        causal = jnp.logical_and(col <= row, same)
        causal_t = jnp.logical_and(row <= col, same)
    else:
        causal = col <= row
        causal_t = row <= col
    ltri = jnp.where(causal, 1.0, 0.0).astype(BF16)
    utri = jnp.where(causal_t, 1.0, 0.0).astype(BF16)

    cum = sum(_dot(ltri, part) for part in _split_bf16(la_ref[...], 3))
    cum_t = _dot_split(lat_ref[...], utri)

    q = q_ref[...]
    kt = kt_ref[...]
    v = v_ref[...]
    vb = v.astype(BF16)
    qd = q * jnp.exp(cum)
    if groups == 1:
        qt = q * jnp.exp(cum - cum[c // 2:c // 2 + 1, :])
        ktt = kt * jnp.exp(cum_t[:, c // 2:c // 2 + 1] - cum_t)
    else:
        qt = qd
        ktt = kt * jnp.exp(-cum_t)
    kttb = ktt.astype(BF16)

    head_l = lax.broadcasted_iota(jnp.int32, (c, GLA_WIDTH), 1) // GLA_DH
    o = jnp.zeros((c, GLA_WIDTH), F32)
    for h in range(GLA_HEADS):
        mh = head_l == h
        att = _dot(jnp.where(mh, qt, 0.0).astype(BF16), kttb)
        att = jnp.where(causal, att, 0.0)
        o = o + _dot(att.astype(BF16), jnp.where(mh, v, 0.0).astype(BF16))

    colg = lax.broadcasted_iota(jnp.int32, (GLA_WIDTH, c), 1) // r
    inter = []
    for gi in range(groups):
        s_old = s_scr[gi]
        inter.append(_dot(qd[gi * r:(gi + 1) * r, :].astype(BF16), s_old.astype(BF16)))
        last_t = cum_t[:, (gi + 1) * r - 1:(gi + 1) * r]
        if groups > 1:
            ing = colg == gi
            kd_t = jnp.where(ing, kt * jnp.exp(jnp.where(ing, last_t - cum_t, 0.0)), 0.0)
        else:
            kd_t = kt * jnp.exp(last_t - cum_t)
        upd = _dot(kd_t.astype(BF16), vb)
        s_scr[gi] = s_old * jnp.exp(last_t) + jnp.where(same_head, upd, 0.0)
    o = o + (inter[0] if groups == 1 else jnp.concatenate(inter, axis=0))

    avg = jnp.where(same_head, 1.0 / GLA_DH, 0.0).astype(BF16)
    msq = _dot_split(o * o, avg, parts=2)
    gg = gg_ref[...]
    o_ref[...] = (o * lax.rsqrt(msq + EPS) * gn_ref[...] * (gg * jax.nn.sigmoid(gg))).astype(BF16)

    @pl.when(step == pl.num_programs(1) - 1)
    def _():
        cr = lax.broadcasted_iota(jnp.int32, (GLA_WIDTH, GLA_DH), 0)
        cc = lax.broadcasted_iota(jnp.int32, (GLA_WIDTH, GLA_DH), 1)
        compact = jnp.where(lax.rem(cr, GLA_DH) == cc, 1.0, 0.0).astype(BF16)
        for gi in range(groups):
            sout_ref[gi] = _dot_split(s_scr[gi], compact)


def _gla(gq, gv, la, gg, kt, lat, s0, gn, *, nseq, t):
    c = LANE
    if t >= c:
        assert t % c == 0
        groups, nb, nsteps = 1, nseq, t // c
    else:
        assert c % t == 0 and nseq % (c // t) == 0
        groups, nb, nsteps = c // t, nseq // (c // t), 1
    m = nseq * t
    row = pl.BlockSpec((c, GLA_WIDTH), lambda b, s: (b * nsteps + s, 0))
    colb = pl.BlockSpec((GLA_WIDTH, c), lambda b, s: (0, b * nsteps + s))
    st = pl.BlockSpec((groups, GLA_WIDTH, GLA_DH), lambda b, s: (b, 0, 0))
    return pl.pallas_call(
        functools.partial(_gla_kernel, c=c, groups=groups),
        grid=(nb, nsteps),
        in_specs=[row, row, row, row, colb, colb, st,
                  pl.BlockSpec(gn.shape, lambda b, s: (0, 0))],
        out_specs=[row, st],
        out_shape=[jax.ShapeDtypeStruct((m, GLA_WIDTH), BF16),
                   jax.ShapeDtypeStruct((nseq, GLA_WIDTH, GLA_DH), F32)],
        scratch_shapes=[pltpu.VMEM((groups, GLA_WIDTH, GLA_WIDTH), F32)],
        compiler_params=_params(2),
        name="gla",
    )(gq, gv, la, gg, kt, lat, s0, gn)


def _layer_weights(l, norm_gain, ffn_w_in, ffn_w_out, w_in, w_out, conv_w, conv_b, conv_ln_g,
                   conv_ln_b, conv_pw, diff_lam, diff_subln_g, gla_w_gate, gla_b_gate, gla_norm_g):
    pad_r = LANE - GLA_RANK
    wga = jnp.pad(w_in[l][:, OFF_GA:], ((0, 0), (0, pad_r))).astype(BF16)
    wgate = jnp.pad(gla_w_gate[l], ((0, pad_r), (0, 0))).astype(BF16)
    win = w_in[l][:, :OFF_GA].astype(BF16)
    dense_in = [norm_gain[l], ffn_w_in[l, 0].astype(BF16), ffn_w_out[l, 0].astype(BF16), win,
                win[:, OFF_GK:OFF_GV].T, wga, wga.T, wgate, wgate.T,
                gla_b_gate[l].reshape(1, GLA_WIDTH), gla_b_gate[l].reshape(GLA_WIDTH, 1)]
    return dict(
        dense_in_s=dense_in,
        dense_in_p=dense_in + [win[:, OFF_DQ:OFF_DK].T, win[:, OFF_DV:OFF_GQ].T],
        dense_out=[norm_gain[l], w_out[l].astype(BF16), ffn_w_in[l, 1].astype(BF16),
                   ffn_w_out[l, 1].astype(BF16)],
        conv=[conv_w[l], conv_b[l].reshape(1, CONV_CH), conv_ln_g[l].reshape(1, CONV_CH),
              conv_ln_b[l].reshape(1, CONV_CH), conv_pw[l].astype(BF16)],
        lam=diff_lam[l], sg=diff_subln_g[l].reshape(1, DIFF_HW),
        gn=jnp.tile(gla_norm_g[l], GLA_HEADS).reshape(1, GLA_WIDTH),
    )


def kernel(x_prompt, x_sample, cache_k, cache_v, page_table, state_conv, state_gla, norm_gain,
           ffn_w_in, ffn_w_out, w_in, w_out, conv_w, conv_b, conv_ln_g, conv_ln_b, conv_pw,
           diff_lam, diff_subln_g, gla_w_gate, gla_b_gate, gla_norm_g):
    bp, tp, d = x_prompt.shape
    bs, ts, _ = x_sample.shape
    depth = norm_gain.shape[0]
    n_pool, page = cache_k.shape[1], cache_k.shape[2]
    ck = cache_k.reshape(depth, n_pool, page * DIFF_HEADS, DIFF_HW)
    cv = cache_v.reshape(depth, n_pool, page * DIFF_HEADS, DIFF_HW)

    xp = x_prompt.reshape(bp * tp, d)
    xs = x_sample.reshape(bs * ts, d)
    conv0 = jnp.zeros((bp, CONV_WIDTH - 1, CONV_CH), F32)
    s0p = jnp.zeros((bp, GLA_WIDTH, GLA_DH), F32)
    kp = jnp.zeros((depth, bp * tp * DIFF_HEADS, DIFF_HW), F32)
    vp = jnp.zeros((depth, bp * tp * DIFF_HEADS, DIFF_HW), F32)
    ksm = jnp.zeros((depth, bs * ts * DIFF_HEADS, DIFF_HW), F32)
    vsm = jnp.zeros((depth, bs * ts * DIFF_HEADS, DIFF_HW), F32)
    outs = [[] for _ in range(4)]
    for l in range(depth):
        w = _layer_weights(l, norm_gain, ffn_w_in, ffn_w_out, w_in, w_out, conv_w, conv_b,
                           conv_ln_g, conv_ln_b, conv_pw, diff_lam, diff_subln_g, gla_w_gate,
                           gla_b_gate, gla_norm_g)
        lam_init = 0.8 - 0.6 * math.exp(-0.3 * l)

        x1, u, kp, vp, gq, gv, la, gg, kt, lat, dqt, kb, vt = _dense_in(
            xp, w["dense_in_p"], kp, vp, layer=l, tm=512, prompt=True)
        co, ncp = _conv(u, conv0, *w["conv"], nb=bp, t=tp)
        do = _attn_prompt(dqt, kb, vt, w["lam"], w["sg"], nb=bp, t=tp, lam_init=lam_init)
        go, sp = _gla(gq, gv, la, gg, kt, lat, s0p, w["gn"], nseq=bp, t=tp)
        xp = _dense_out(x1, co, do, go, *w["dense_out"], tm=512)
        outs[0].append(ncp)
        outs[1].append(sp.reshape(bp, GLA_HEADS, GLA_DH, GLA_DH))

        x1, u, ksm, vsm, gq, gv, la, gg, kt, lat, dq = _dense_in(
            xs, w["dense_in_s"], ksm, vsm, layer=l, tm=256, prompt=False)
        co, ncs = _conv(u, state_conv[l], *w["conv"], nb=bs, t=ts)
        do = _attn_sample(page_table, dq, ksm, vsm, w["lam"], w["sg"], ck, cv, layer=l, nb=bs,
                          ts=ts, lam_init=lam_init)
        go, ss = _gla(gq, gv, la, gg, kt, lat, state_gla[l].reshape(bs, GLA_WIDTH, GLA_DH),
                      w["gn"], nseq=bs, t=ts)
        xs = _dense_out(x1, co, do, go, *w["dense_out"], tm=512)
        outs[2].append(ncs)
        outs[3].append(ss.reshape(bs, GLA_HEADS, GLA_DH, GLA_DH))

    kv_shape_p = (depth, bp, tp, DIFF_HEADS, DIFF_HW)
    kv_shape_s = (depth, bs, ts, DIFF_HEADS, DIFF_HW)
    ncp, sp, ncs, ss = (jnp.stack(o) for o in outs)
    return (xp.reshape(bp, tp, d), xs.reshape(bs, ts, d), kp.reshape(kv_shape_p),
            vp.reshape(kv_shape_p), ncp, sp, ksm.reshape(kv_shape_s), vsm.reshape(kv_shape_s), ncs, ss)
```

```python
import functools
import math

import jax
import jax.numpy as jnp
from jax import lax
from jax.experimental import pallas as pl
from jax.experimental.pallas import tpu as pltpu

F32 = jnp.float32
BF16 = jnp.bfloat16
EPS = 1e-6
LOG2E = math.log2(math.e)

CONV_CH = 256
CONV_WIDTH = 31
CONV_HALO = 32
DIFF_HEADS = 4
DIFF_DH = 64
DIFF_HW = 2 * DIFF_DH
DIFF_WIDTH = DIFF_HEADS * DIFF_HW
GLA_HEADS = 4
GLA_DH = 64
GLA_WIDTH = GLA_HEADS * GLA_DH
GLA_RANK = 16
GLA_TAU = 16.0
OFF_DQ = 2 * CONV_CH
OFF_DK = OFF_DQ + DIFF_WIDTH
OFF_DV = OFF_DK + DIFF_WIDTH
OFF_GQ = OFF_DV + DIFF_WIDTH
OFF_GK = OFF_GQ + GLA_WIDTH
OFF_GV = OFF_GK + GLA_WIDTH
OFF_GG = OFF_GV + GLA_WIDTH
OFF_GA = OFF_GG + GLA_WIDTH

LANE = 128
SUBLANE = 8
MXU_N = 256
ATTN_ROWS = 64
ONES_ROWS = 16
VMEM_LIMIT = 56 * 1024 * 1024

NT_DIMS = (((1,), (1,)), ((), ()))


def _dot(a, b):
    return jnp.dot(a, b, preferred_element_type=F32)


def _dot_nt(a, b):
    return lax.dot_general(a, b, NT_DIMS, preferred_element_type=F32)


def _rms(x, g):
    return x * lax.rsqrt(jnp.mean(x * x, axis=-1, keepdims=True) + EPS) * g


def _log_sigmoid(x):
    return jnp.minimum(x, 0.0) - jnp.log1p(jnp.exp(-jnp.abs(x)))


def _split_bf16(x, parts):
    out = []
    r = x
    for _ in range(parts - 1):
        p = r.astype(BF16)
        out.append(p)
        r = r - p.astype(F32)
    out.append(r.astype(BF16))
    return out


def _dot_split(x, w01, parts=3):
    return sum(_dot(p, w01) for p in _split_bf16(x, parts))


def _params(n_axes):
    return pltpu.CompilerParams(dimension_semantics=("arbitrary",) * n_axes,
                                vmem_limit_bytes=VMEM_LIMIT)


def _resident(shape):
    nd = len(shape)
    return pl.BlockSpec(shape, lambda *_: (0,) * nd, pipeline_mode=pl.Buffered(1))


def _ffn(x, g_pre, g_post, wgu_ref, wo_ref, a_scr, d_ff, chunk):
    h = _rms(x, g_pre).astype(BF16)
    for c in range(0, d_ff, chunk):
        gate = _dot(h, wgu_ref[:, c:c + chunk])
        up = _dot(h, wgu_ref[:, d_ff + c:d_ff + c + chunk])
        a_scr[:, c:c + chunk] = (gate * jax.nn.sigmoid(gate) * up).astype(BF16)
    y = _dot(a_scr[...], wo_ref[...])
    return x + 0.5 * _rms(y, g_post)


def _dense_in_kernel(*refs, d_ff, chunk, prompt):
    if prompt:
        (x_ref, g_ref, wgu_ref, wo_ref, win_ref, wkt_ref, wga_ref, wgate_ref, bg_ref,
         wdqt_ref, wvt_ref, _, _,
         x1_ref, u_ref, kn_ref, vn_ref, gq_ref, gv_ref, la_ref, gg_ref, kt_ref,
         dqt_ref, kb_ref, vt_ref, a_scr) = refs
    else:
        (x_ref, g_ref, wgu_ref, wo_ref, win_ref, wkt_ref, wga_ref, wgate_ref, bg_ref, _, _,
         x1_ref, u_ref, kn_ref, vn_ref, gq_ref, gv_ref, la_ref, gg_ref, kt_ref,
         dq_ref, a_scr) = refs
    g = g_ref[...]
    x1 = _ffn(x_ref[...], g[0:1], g[1:2], wgu_ref, wo_ref, a_scr, d_ff, chunk)
    x1_ref[...] = x1
    h = _rms(x1, g[2:3]).astype(BF16)

    ca = _dot(h, win_ref[:, 0:CONV_CH])
    cg = _dot(h, win_ref[:, CONV_CH:OFF_DQ])
    u_ref[...] = ca * jax.nn.sigmoid(cg)

    k = _dot(h, win_ref[:, OFF_DK:OFF_DV])
    v = _dot(h, win_ref[:, OFF_DV:OFF_GQ])
    tm = k.shape[0]
    for hd in range(DIFF_HEADS):
        lanes = slice(hd * DIFF_HW, (hd + 1) * DIFF_HW)
        kn_ref[pl.ds(hd, tm, stride=DIFF_HEADS), :] = k[:, lanes]
        vn_ref[pl.ds(hd, tm, stride=DIFF_HEADS), :] = v[:, lanes]
    if prompt:
        dqt_ref[...] = (_dot_nt(wdqt_ref[...], h) * (DIFF_DH ** -0.5 * LOG2E)).astype(BF16)
        kb_ref[...] = k.astype(BF16)
        vt_ref[...] = _dot_nt(wvt_ref[...], h).astype(BF16)
    else:
        dq_ref[...] = (_dot(h, win_ref[:, OFF_DQ:OFF_DK]) * (DIFF_DH ** -0.5)).astype(BF16)

    gq_ref[...] = _dot(h, win_ref[:, OFF_GQ:OFF_GK]) * (GLA_DH ** -0.5)
    gv_ref[...] = _dot(h, win_ref[:, OFF_GV:OFF_GG])
    gg_ref[...] = _dot(h, win_ref[:, OFF_GG:OFF_GA])
    ga = _dot(h, wga_ref[...])
    a_logit = _dot(ga.astype(BF16), wgate_ref[...]) + bg_ref[...]
    la_ref[...] = _log_sigmoid(a_logit) * (1.0 / GLA_TAU)

    kt_ref[...] = _dot_nt(wkt_ref[...], h)


def _dense_in(x, weights, k_all, v_all, *, layer, tm, prompt):
    m, d = x.shape
    d_ff = weights[2].shape[0]
    chunk = MXU_N
    assert m % tm == 0 and d_ff % chunk == 0
    row = lambda w: pl.BlockSpec((tm, w), lambda i: (i, 0))
    col = lambda h: pl.BlockSpec((h, tm), lambda i: (0, i))
    rows = lambda w, dt: (jax.ShapeDtypeStruct((m, w), dt), row(w))
    cols = lambda h, dt: (jax.ShapeDtypeStruct((h, m), dt), col(h))
    kv = lambda a: (jax.ShapeDtypeStruct(a.shape, a.dtype),
                    pl.BlockSpec((None, tm * DIFF_HEADS, DIFF_HW), lambda i: (layer, i, 0)))
    outs = [rows(d, F32),
            rows(CONV_CH, F32),
            kv(k_all), kv(v_all),
            rows(GLA_WIDTH, F32), rows(GLA_WIDTH, F32),
            rows(GLA_WIDTH, F32), rows(GLA_WIDTH, F32),
            cols(GLA_WIDTH, F32)]
    if prompt:
        outs += [cols(DIFF_WIDTH, BF16), rows(DIFF_WIDTH, BF16), cols(DIFF_WIDTH, BF16)]
    else:
        outs += [rows(DIFF_WIDTH, BF16)]
    return pl.pallas_call(
        functools.partial(_dense_in_kernel, d_ff=d_ff, chunk=chunk, prompt=prompt),
        grid=(m // tm,),
        in_specs=[row(d)] + [_resident(w.shape) for w in weights]
        + [pl.BlockSpec(memory_space=pl.ANY)] * 2,
        out_specs=[s for _, s in outs],
        out_shape=[s for s, _ in outs],
        input_output_aliases={1 + len(weights): 2, 2 + len(weights): 3},
        scratch_shapes=[pltpu.VMEM((tm, d_ff), BF16)],
        compiler_params=_params(1),
        name="dense_in",
    )(x, *weights, k_all, v_all)


def _dense_out_kernel(x_ref, co_ref, do_ref, go_ref, g_ref, wout_ref, wgu_ref, wo_ref,
                      x2_ref, a_scr, *, d_ff, chunk):
    g = g_ref[...]
    m = (_dot(co_ref[...], wout_ref[0:CONV_CH, :])
         + _dot(do_ref[...], wout_ref[CONV_CH:CONV_CH + DIFF_WIDTH, :])
         + _dot(go_ref[...], wout_ref[CONV_CH + DIFF_WIDTH:, :]))
    x = x_ref[...] + _rms(m, g[3:4])
    x2_ref[...] = _ffn(x, g[4:5], g[5:6], wgu_ref, wo_ref, a_scr, d_ff, chunk)


def _dense_out(x, co, do, go, g, wout, wgu, wo, *, tm):
    m, d = x.shape
    d_ff = wo.shape[0]
    chunk = MXU_N
    assert m % tm == 0 and d_ff % chunk == 0
    row = lambda w: pl.BlockSpec((tm, w), lambda i: (i, 0))
    weights = (g, wout, wgu, wo)
    return pl.pallas_call(
        functools.partial(_dense_out_kernel, d_ff=d_ff, chunk=chunk),
        grid=(m // tm,),
        in_specs=[row(d), row(CONV_CH), row(DIFF_WIDTH), row(GLA_WIDTH)]
        + [_resident(w.shape) for w in weights],
        out_specs=row(d),
        out_shape=jax.ShapeDtypeStruct((m, d), F32),
        scratch_shapes=[pltpu.VMEM((tm, d_ff), BF16)],
        compiler_params=_params(1),
        name="dense_out",
    )(x, co, do, go, *weights)


def _conv_kernel(u_ref, buf_ref, w_ref, b_ref, lg_ref, lb_ref, pw_ref, o_ref, nc_ref, scr,
                 *, nseq, t, tt):
    hist = CONV_WIDTH - 1
    lead = CONV_HALO - hist
    w = w_ref[...]
    bias = b_ref[...]
    lg = lg_ref[...]
    lb = lb_ref[...]
    pw = pw_ref[...]
    scr[0:lead, :] = jnp.zeros((lead, CONV_CH), F32)

    for s in range(nseq):
        scr[lead:CONV_HALO, :] = buf_ref[s]
        scr[CONV_HALO:CONV_HALO + t, :] = u_ref[s]
        nc_ref[s] = scr[t + lead:t + CONV_HALO, :]

        def tile(i, carry):
            t0 = pl.multiple_of(i * tt, tt)
            win = scr[pl.ds(t0, tt + CONV_HALO), :]
            acc = jnp.zeros((tt, CONV_CH), F32) + bias
            for r in range(SUBLANE):
                taps = [j for j in range(CONV_WIDTH) if (lead + j) % SUBLANE == r]
                sh = win if r == 0 else win[r:r + tt + CONV_HALO - SUBLANE, :]
                for j in taps:
                    a = lead + j - r
                    acc = acc + w[j:j + 1, :] * sh[a:a + tt, :]
            mu = jnp.mean(acc, axis=-1, keepdims=True)
            xc = acc - mu
            y = xc * lax.rsqrt(jnp.mean(xc * xc, axis=-1, keepdims=True) + EPS) * lg + lb
            y = y * jax.nn.sigmoid(y)
            o_ref[s, pl.ds(t0, tt), :] = _dot(y.astype(BF16), pw).astype(BF16)
            return carry

        lax.fori_loop(0, t // tt, tile, 0)


def _conv(u, buf, w, b, lg, lb, pw, *, nb, t):
    tt = min(t, 256)
    nseq = 1 if t >= 256 else min(nb, 16)
    assert t % tt == 0 and tt % SUBLANE == 0 and nb % nseq == 0
    hist = CONV_WIDTH - 1
    u3 = u.reshape(nb, t, CONV_CH)
    small = lambda a: _resident(a.shape)
    out, new_conv = pl.pallas_call(
        functools.partial(_conv_kernel, nseq=nseq, t=t, tt=tt),
        grid=(nb // nseq,),
        in_specs=[pl.BlockSpec((nseq, t, CONV_CH), lambda i: (i, 0, 0)),
                  pl.BlockSpec((nseq, hist, CONV_CH), lambda i: (i, 0, 0)),
                  small(w), small(b), small(lg), small(lb), small(pw)],
        out_specs=[pl.BlockSpec((nseq, t, CONV_CH), lambda i: (i, 0, 0)),
                   pl.BlockSpec((nseq, hist, CONV_CH), lambda i: (i, 0, 0))],
        out_shape=[jax.ShapeDtypeStruct((nb, t, CONV_CH), BF16),
                   jax.ShapeDtypeStruct((nb, hist, CONV_CH), F32)],
        scratch_shapes=[pltpu.VMEM((CONV_HALO + t, CONV_CH), F32)],
        compiler_params=_params(1),
        name="conv",
    )(u3, buf, w, b, lg, lb, pw)
    return out.reshape(nb * t, CONV_CH), new_conv


def _fold_rows(x, op):
    n = x.shape[0]
    while n > SUBLANE:
        n //= 2
        x = op(x[:n], x[n:2 * n])
    return x


def _fold_list(xs, op):
    while len(xs) > 1:
        xs = [op(xs[i], xs[i + 1]) for i in range(0, len(xs) - 1, 2)] + (xs[-1:] if len(xs) % 2 else [])
    return xs[0]


def _lam(lam_ref, lam_init):
    dl = lam_ref[...]
    s1 = jnp.sum(dl[0:1] * dl[1:2], axis=-1, keepdims=True)
    s2 = jnp.sum(dl[2:3] * dl[3:4], axis=-1, keepdims=True)
    return jnp.exp(s1) - jnp.exp(s2) + lam_init


def _attn_p_kernel(qt_ref, k_ref, vt_ref, lam_ref, sg_ref, o_ref, m_scr, acc_scr,
                   qs_scr, sa_scr, sb_scr, *, tq, tk, lam_init):
    qi = pl.program_id(2)
    qt = qt_ref[...].astype(F32)
    sub = lax.broadcasted_iota(jnp.int32, (DIFF_HW, tq), 0)
    qs_scr[...] = jnp.concatenate([jnp.where(sub < DIFF_DH, qt, 0.0),
                                   jnp.where(sub >= DIFF_DH, qt, 0.0)], axis=1).astype(BF16)
    m_scr[...] = jnp.full(m_scr.shape, -jnp.inf, F32)
    acc_scr[...] = jnp.zeros(acc_scr.shape, F32)

    tiles = [slice(c * MXU_N, (c + 1) * MXU_N) for c in range(2 * tq // MXU_N)]

    def live_tiles(diag):
        if diag is None:
            return tiles
        return [cs for cs in tiles if cs.start % tq + MXU_N - 1 >= diag * tk]

    def scores(j, s_ref, diag=None):
        k0 = pl.multiple_of(j * tk, tk)
        kb = k_ref[pl.ds(k0, tk), :]
        for cs in live_tiles(diag):
            s_ref[:, cs] = _dot(kb, qs_scr[:, cs])

    def softmax_pv(j, s_ref, diag=None):
        k0 = pl.multiple_of(j * tk, tk)
        vtb = jnp.concatenate([vt_ref[:, pl.ds(k0, tk)], jnp.ones((ONES_ROWS, tk), BF16)], axis=0)
        chunks = range(0, tk, ATTN_ROWS)
        for cs in live_tiles(diag):
            q0 = cs.start % tq
            masked = diag is not None and q0 < (diag + 1) * tk - 1

            def chunk(r0):
                s = s_ref[r0:r0 + ATTN_ROWS, cs]
                if masked:
                    key = lax.broadcasted_iota(jnp.int32, (ATTN_ROWS, MXU_N), 0) + (diag * tk + r0)
                    qry = lax.broadcasted_iota(jnp.int32, (ATTN_ROWS, MXU_N), 1) + q0
                    s = jnp.where(key <= qry, s, -jnp.inf)
                return s

            m_blk = _fold_list([_fold_rows(chunk(r0), jnp.maximum) for r0 in chunks], jnp.maximum)
            m_prev = m_scr[:, cs]
            m_new = jnp.maximum(m_prev, jnp.max(m_blk, axis=0, keepdims=True))
            alpha = jnp.exp2(m_prev - m_new)
            p = jnp.concatenate([jnp.exp2(chunk(r0) - m_new).astype(BF16) for r0 in chunks], axis=0)
            acc_scr[:, cs] = alpha * acc_scr[:, cs] + _dot(vtb, p)
            m_scr[:, cs] = m_new

    scores(0, sa_scr)

    def body(i, carry):
        j = 2 * i
        scores(j + 1, sb_scr)
        softmax_pv(j, sa_scr)
        scores(j + 2, sa_scr)
        softmax_pv(j + 1, sb_scr)
        return carry

    lax.fori_loop(0, qi, body, 0)
    scores(2 * qi + 1, sb_scr, diag=1)
    softmax_pv(2 * qi, sa_scr, diag=0)
    softmax_pv(2 * qi + 1, sb_scr, diag=1)

    o = acc_scr[0:DIFF_HW, :] / acc_scr[DIFF_HW:DIFF_HW + 1, :]
    lam = _lam(lam_ref, lam_init)
    o = (o[:, 0:tq] - lam * o[:, tq:2 * tq]).T
    o_ref[...] = (_rms(o, sg_ref[...]) * (1.0 - lam_init)).astype(BF16)


def _attn_prompt(dqt, kb, vt, lam_p, sg, *, nb, t, lam_init):
    tq = 1024
    tk = tq // 2
    assert t % tq == 0 and tk % MXU_N == 0 and tk % ATTN_ROWS == 0
    nq = t // tq
    return pl.pallas_call(
        functools.partial(_attn_p_kernel, tq=tq, tk=tk, lam_init=lam_init),
        grid=(nb, DIFF_HEADS, nq),
        in_specs=[pl.BlockSpec((DIFF_HW, tq), lambda b, h, i: (h, b * nq + i)),
                  pl.BlockSpec((t, DIFF_HW), lambda b, h, i: (b, h)),
                  pl.BlockSpec((DIFF_HW, t), lambda b, h, i: (h, b)),
                  _resident(lam_p.shape), _resident(sg.shape)],
        out_specs=pl.BlockSpec((tq, DIFF_HW), lambda b, h, i: (b * nq + i, h)),
        out_shape=jax.ShapeDtypeStruct((nb * t, DIFF_WIDTH), BF16),
        scratch_shapes=[pltpu.VMEM((1, 2 * tq), F32),
                        pltpu.VMEM((DIFF_HW + ONES_ROWS, 2 * tq), F32),
                        pltpu.VMEM((DIFF_HW, 2 * tq), BF16),
                        pltpu.VMEM((tk, 2 * tq), F32),
                        pltpu.VMEM((tk, 2 * tq), F32)],
        compiler_params=_params(3),
        name="attn_prompt",
    )(dqt, kb, vt, lam_p, sg)


def _attn_s_kernel(pt_ref, q_ref, kn_ref, vn_ref, lam_ref, sg_ref, ck_hbm, cv_hbm, o_ref,
                   kbuf, vbuf, bias, sem, *, layer, n_pages, prow, ts, lam_init):
    b = pl.program_id(0)
    nb = pl.num_programs(0)
    slot = lax.rem(b, 2)
    past = n_pages * prow
    new = ts * DIFF_HEADS
    rows = kbuf.shape[1]
    nq = 2 * DIFF_HEADS * ts

    def page_copies(bb, sl):
        cps = []
        for p in range(n_pages):
            pg = pt_ref[bb, p]
            dst = pl.ds(p * prow, prow)
            cps.append(pltpu.make_async_copy(ck_hbm.at[layer, pg], kbuf.at[sl, dst], sem.at[0, sl]))
            cps.append(pltpu.make_async_copy(cv_hbm.at[layer, pg], vbuf.at[sl, dst], sem.at[1, sl]))
        return cps

    def start_all(cps):
        for i, cp in enumerate(cps):
            cp.start(priority=i % 2)

    @pl.when(b == 0)
    def _():
        for sl in range(2):
            kbuf[sl, past:rows, :] = jnp.zeros((rows - past, DIFF_HW), F32)
            vbuf[sl, past:rows, :] = jnp.zeros((rows - past, DIFF_HW), F32)
        r = lax.broadcasted_iota(jnp.int32, (nq, rows), 0)
        c = lax.broadcasted_iota(jnp.int32, (nq, rows), 1)
        same_head = lax.rem(c, DIFF_HEADS) == r // (2 * ts)
        causal = c // DIFF_HEADS - past // DIFF_HEADS <= lax.rem(r, ts)
        bias[...] = jnp.where(jnp.logical_and(same_head, causal), 0.0, -jnp.inf)
        start_all(page_copies(0, 0))

    @pl.when(b + 1 < nb)
    def _():
        start_all(page_copies(b + 1, 1 - slot))

    for cp in page_copies(b, slot):
        cp.wait()

    kbuf[slot, past:past + new, :] = kn_ref[...]
    vbuf[slot, past:past + new, :] = vn_ref[...]

    q = q_ref[...].astype(F32)
    half = lax.broadcasted_iota(jnp.int32, (ts, DIFF_HW), 1) // DIFF_DH
    qrows = []
    for h in range(DIFF_HEADS):
        qh = q[:, h * DIFF_HW:(h + 1) * DIFF_HW]
        qrows += [jnp.where(half == 0, qh, 0.0), jnp.where(half == 1, qh, 0.0)]
    qrows = jnp.concatenate(qrows, axis=0).astype(BF16)
    s = _dot_nt(qrows, kbuf[slot].astype(BF16)) + bias[...]
    m = jnp.max(s, axis=-1, keepdims=True)
    p = jnp.exp(s - m)
    l = jnp.sum(p, axis=-1, keepdims=True)
    pv = _dot(p.astype(BF16), vbuf[slot].astype(BF16)) / l
    lam = _lam(lam_ref, lam_init)
    sg = sg_ref[...]
    outs = []
    for h in range(DIFF_HEADS):
        o = pv[2 * h * ts:(2 * h + 1) * ts] - lam * pv[(2 * h + 1) * ts:(2 * h + 2) * ts]
        outs.append(_rms(o, sg) * (1.0 - lam_init))
    o_ref[...] = jnp.concatenate(outs, axis=-1).astype(BF16)


def _attn_sample(page_table, dq, kn, vn, lam_p, sg, ck, cv, *, layer, nb, ts, lam_init):
    n_pages = page_table.shape[1]
    prow = ck.shape[2]
    new = ts * DIFF_HEADS
    rows = n_pages * prow + LANE
    nq = 2 * DIFF_HEADS * ts
    assert new <= LANE and ts % SUBLANE == 0
    const = lambda a: pl.BlockSpec(a.shape, lambda b, pt: (0,) * a.ndim, pipeline_mode=pl.Buffered(1))
    grid_spec = pltpu.PrefetchScalarGridSpec(
        num_scalar_prefetch=1,
        grid=(nb,),
        in_specs=[pl.BlockSpec((ts, DIFF_WIDTH), lambda b, pt: (b, 0)),
                  pl.BlockSpec((None, new, DIFF_HW), lambda b, pt: (layer, b, 0)),
                  pl.BlockSpec((None, new, DIFF_HW), lambda b, pt: (layer, b, 0)),
                  const(lam_p), const(sg),
                  pl.BlockSpec(memory_space=pl.ANY), pl.BlockSpec(memory_space=pl.ANY)],
        out_specs=pl.BlockSpec((ts, DIFF_WIDTH), lambda b, pt: (b, 0)),
        scratch_shapes=[pltpu.VMEM((2, rows, DIFF_HW), F32),
                        pltpu.VMEM((2, rows, DIFF_HW), F32),
                        pltpu.VMEM((nq, rows), F32),
                        pltpu.SemaphoreType.DMA((2, 2))],
    )
    return pl.pallas_call(
        functools.partial(_attn_s_kernel, layer=layer, n_pages=n_pages, prow=prow, ts=ts,
                          lam_init=lam_init),
        grid_spec=grid_spec,
        out_shape=jax.ShapeDtypeStruct((nb * ts, DIFF_WIDTH), BF16),
        compiler_params=_params(1),
        name="attn_sample",
    )(page_table, dq, kn, vn, lam_p, sg, ck, cv)


def _gla_kernel(q_ref, v_ref, la_ref, gg_ref, kt_ref, s0_ref, gn_ref,
                o_ref, sout_ref, s_scr, *, c, groups):
    step = pl.program_id(1)
    r = c // groups

    hr = lax.broadcasted_iota(jnp.int32, (GLA_WIDTH, GLA_WIDTH), 0) // GLA_DH
    hc = lax.broadcasted_iota(jnp.int32, (GLA_WIDTH, GLA_WIDTH), 1) // GLA_DH
    same_head = hr == hc

    @pl.when(step == 0)
    def _():
        er = lax.broadcasted_iota(jnp.int32, (GLA_DH, GLA_WIDTH), 0)
        ec = lax.broadcasted_iota(jnp.int32, (GLA_DH, GLA_WIDTH), 1)
        expand = jnp.where(lax.rem(ec, GLA_DH) == er, 1.0, 0.0).astype(BF16)
        for gi in range(groups):
            s_scr[gi] = jnp.where(same_head, _dot_split(s0_ref[gi], expand), 0.0)

    row = lax.broadcasted_iota(jnp.int32, (c, c), 0)
    col = lax.broadcasted_iota(jnp.int32, (c, c), 1)
    if groups > 1:
        causal = jnp.logical_and(col <= row, (row // r) == (col // r))
    else:
        causal = col <= row
    ltri = jnp.where(causal, 1.0, 0.0).astype(BF16)

    cum = sum(_dot(ltri, part) for part in _split_bf16(la_ref[...], 3))
    cum_t = cum.T

    q = q_ref[...]
    kt = kt_ref[...]
    v = v_ref[...]
    vb = v.astype(BF16)
    qd = q * jnp.exp(cum)
    if groups == 1:
        qt = q * jnp.exp(cum - cum[c // 2:c // 2 + 1, :])
        ktt = kt * jnp.exp(cum_t[:, c // 2:c // 2 + 1] - cum_t)
    else:
        qt = qd
        ktt = kt * jnp.exp(-cum_t)
    kttb = ktt.astype(BF16)

    head_l = lax.broadcasted_iota(jnp.int32, (c, GLA_WIDTH), 1) // GLA_DH
    heads = range(GLA_HEADS)
    q4 = jnp.concatenate([jnp.where(head_l == h, qt, 0.0) for h in heads], axis=0).astype(BF16)
    att4 = _dot(q4, kttb)
    att = jnp.concatenate([jnp.where(causal, att4[h * c:(h + 1) * c], 0.0) for h in heads],
                          axis=1).astype(BF16)
    v4 = jnp.concatenate([jnp.where(head_l == h, v, 0.0) for h in heads], axis=0).astype(BF16)
    o = _dot(att, v4)

    colg = lax.broadcasted_iota(jnp.int32, (GLA_WIDTH, c), 1) // r
    inter = []
    for gi in range(groups):
        s_old = s_scr[gi]
        inter.append(_dot(qd[gi * r:(gi + 1) * r, :].astype(BF16), s_old.astype(BF16)))
        last_t = cum_t[:, (gi + 1) * r - 1:(gi + 1) * r]
        if groups > 1:
            ing = colg == gi
            kd_t = jnp.where(ing, kt * jnp.exp(jnp.where(ing, last_t - cum_t, 0.0)), 0.0)
        else:
            kd_t = kt * jnp.exp(last_t - cum_t)
        upd = _dot(kd_t.astype(BF16), vb)
        s_scr[gi] = s_old * jnp.exp(last_t) + jnp.where(same_head, upd, 0.0)
    o = o + (inter[0] if groups == 1 else jnp.concatenate(inter, axis=0))

    avg = jnp.where(same_head, 1.0 / GLA_DH, 0.0).astype(BF16)
    msq = _dot_split(o * o, avg, parts=2)
    gg = gg_ref[...]
    o_ref[...] = (o * lax.rsqrt(msq + EPS) * gn_ref[...] * (gg * jax.nn.sigmoid(gg))).astype(BF16)

    @pl.when(step == pl.num_programs(1) - 1)
    def _():
        cr = lax.broadcasted_iota(jnp.int32, (GLA_WIDTH, GLA_DH), 0)
        cc = lax.broadcasted_iota(jnp.int32, (GLA_WIDTH, GLA_DH), 1)
        compact = jnp.where(lax.rem(cr, GLA_DH) == cc, 1.0, 0.0).astype(BF16)
        for gi in range(groups):
            sout_ref[gi] = _dot_split(s_scr[gi], compact)


def _gla(gq, gv, la, gg, kt, s0, gn, *, nseq, t):
    c = LANE
    if t >= c:
        assert t % c == 0
        groups, nb, nsteps = 1, nseq, t // c
    else:
        assert c % t == 0 and nseq % (c // t) == 0
        groups, nb, nsteps = c // t, nseq // (c // t), 1
    m = nseq * t
    row = pl.BlockSpec((c, GLA_WIDTH), lambda b, s: (b * nsteps + s, 0))
    colb = pl.BlockSpec((GLA_WIDTH, c), lambda b, s: (0, b * nsteps + s))
    st = pl.BlockSpec((groups, GLA_WIDTH, GLA_DH), lambda b, s: (b, 0, 0))
    return pl.pallas_call(
        functools.partial(_gla_kernel, c=c, groups=groups),
        grid=(nb, nsteps),
        in_specs=[row, row, row, row, colb, st,
                  pl.BlockSpec(gn.shape, lambda b, s: (0, 0))],
        out_specs=[row, st],
        out_shape=[jax.ShapeDtypeStruct((m, GLA_WIDTH), BF16),
                   jax.ShapeDtypeStruct((nseq, GLA_WIDTH, GLA_DH), F32)],
        scratch_shapes=[pltpu.VMEM((groups, GLA_WIDTH, GLA_WIDTH), F32)],
        compiler_params=_params(2),
        name="gla",
    )(gq, gv, la, gg, kt, s0, gn)


def _layer_weights(l, norm_gain, ffn_w_in, ffn_w_out, w_in, w_out, conv_w, conv_b, conv_ln_g,
                   conv_ln_b, conv_pw, diff_lam, diff_subln_g, gla_w_gate, gla_b_gate, gla_norm_g):
    pad_r = LANE - GLA_RANK
    wga = jnp.pad(w_in[l][:, OFF_GA:], ((0, 0), (0, pad_r))).astype(BF16)
    wgate = jnp.pad(gla_w_gate[l], ((0, pad_r), (0, 0))).astype(BF16)
    win = w_in[l][:, :OFF_GA].astype(BF16)
    dense_in = [norm_gain[l], ffn_w_in[l, 0].astype(BF16), ffn_w_out[l, 0].astype(BF16), win,
                win[:, OFF_GK:OFF_GV].T, wga, wgate, gla_b_gate[l].reshape(1, GLA_WIDTH)]
    return dict(
        dense_in_s=dense_in,
        dense_in_p=dense_in + [win[:, OFF_DQ:OFF_DK].T, win[:, OFF_DV:OFF_GQ].T],
        dense_out=[norm_gain[l], w_out[l].astype(BF16), ffn_w_in[l, 1].astype(BF16),
                   ffn_w_out[l, 1].astype(BF16)],
        conv=[conv_w[l], conv_b[l].reshape(1, CONV_CH), conv_ln_g[l].reshape(1, CONV_CH),
              conv_ln_b[l].reshape(1, CONV_CH), conv_pw[l].astype(BF16)],
        lam=diff_lam[l], sg=diff_subln_g[l].reshape(1, DIFF_HW),
        gn=jnp.tile(gla_norm_g[l], GLA_HEADS).reshape(1, GLA_WIDTH),
    )


def kernel(x_prompt, x_sample, cache_k, cache_v, page_table, state_conv, state_gla, norm_gain,
           ffn_w_in, ffn_w_out, w_in, w_out, conv_w, conv_b, conv_ln_g, conv_ln_b, conv_pw,
           diff_lam, diff_subln_g, gla_w_gate, gla_b_gate, gla_norm_g):
    bp, tp, d = x_prompt.shape
    bs, ts, _ = x_sample.shape
    depth = norm_gain.shape[0]
    n_pool, page = cache_k.shape[1], cache_k.shape[2]
    ck = cache_k.reshape(depth, n_pool, page * DIFF_HEADS, DIFF_HW)
    cv = cache_v.reshape(depth, n_pool, page * DIFF_HEADS, DIFF_HW)

    xp = x_prompt.reshape(bp * tp, d)
    xs = x_sample.reshape(bs * ts, d)
    conv0 = jnp.zeros((bp, CONV_WIDTH - 1, CONV_CH), F32)
    s0p = jnp.zeros((bp, GLA_WIDTH, GLA_DH), F32)
    kp = jnp.zeros((depth, bp * tp * DIFF_HEADS, DIFF_HW), F32)
    vp = jnp.zeros((depth, bp * tp * DIFF_HEADS, DIFF_HW), F32)
    ksm = jnp.zeros((depth, bs * ts * DIFF_HEADS, DIFF_HW), F32)
    vsm = jnp.zeros((depth, bs * ts * DIFF_HEADS, DIFF_HW), F32)
    outs = [[] for _ in range(4)]
    for l in range(depth):
        w = _layer_weights(l, norm_gain, ffn_w_in, ffn_w_out, w_in, w_out, conv_w, conv_b,
                           conv_ln_g, conv_ln_b, conv_pw, diff_lam, diff_subln_g, gla_w_gate,
                           gla_b_gate, gla_norm_g)
        lam_init = 0.8 - 0.6 * math.exp(-0.3 * l)

        x1, u, kp, vp, gq, gv, la, gg, kt, dqt, kb, vt = _dense_in(
            xp, w["dense_in_p"], kp, vp, layer=l, tm=512, prompt=True)
        co, ncp = _conv(u, conv0, *w["conv"], nb=bp, t=tp)
        do = _attn_prompt(dqt, kb, vt, w["lam"], w["sg"], nb=bp, t=tp, lam_init=lam_init)
        go, sp = _gla(gq, gv, la, gg, kt, s0p, w["gn"], nseq=bp, t=tp)
        xp = _dense_out(x1, co, do, go, *w["dense_out"], tm=512)
        outs[0].append(ncp)
        outs[1].append(sp.reshape(bp, GLA_HEADS, GLA_DH, GLA_DH))

        x1, u, ksm, vsm, gq, gv, la, gg, kt, dq = _dense_in(
            xs, w["dense_in_s"], ksm, vsm, layer=l, tm=256, prompt=False)
        co, ncs = _conv(u, state_conv[l], *w["conv"], nb=bs, t=ts)
        do = _attn_sample(page_table, dq, ksm, vsm, w["lam"], w["sg"], ck, cv, layer=l, nb=bs,
                          ts=ts, lam_init=lam_init)
        go, ss = _gla(gq, gv, la, gg, kt, state_gla[l].reshape(bs, GLA_WIDTH, GLA_DH),
                      w["gn"], nseq=bs, t=ts)
        xs = _dense_out(x1, co, do, go, *w["dense_out"], tm=512)
        outs[2].append(ncs)
        outs[3].append(ss.reshape(bs, GLA_HEADS, GLA_DH, GLA_DH))

    kv_shape_p = (depth, bp, tp, DIFF_HEADS, DIFF_HW)
    kv_shape_s = (depth, bs, ts, DIFF_HEADS, DIFF_HW)
    ncp, sp, ncs, ss = (jnp.stack(o) for o in outs)
    return (xp.reshape(bp, tp, d), xs.reshape(bs, ts, d), kp.reshape(kv_shape_p),
            vp.reshape(kv_shape_p), ncp, sp, ksm.reshape(kv_shape_s), vsm.reshape(kv_shape_s), ncs, ss)
```

```python
import functools
import math

import jax
import jax.numpy as jnp
from jax import lax
from jax.experimental import pallas as pl
from jax.experimental.pallas import tpu as pltpu

F32 = jnp.float32
BF16 = jnp.bfloat16
EPS = 1e-6
LOG2E = math.log2(math.e)

CONV_CH = 256
CONV_WIDTH = 31
CONV_HALO = 32
DIFF_HEADS = 4
DIFF_DH = 64
DIFF_HW = 2 * DIFF_DH
DIFF_WIDTH = DIFF_HEADS * DIFF_HW
GLA_HEADS = 4
GLA_DH = 64
GLA_WIDTH = GLA_HEADS * GLA_DH
GLA_RANK = 16
GLA_TAU = 16.0
OFF_DQ = 2 * CONV_CH
OFF_DK = OFF_DQ + DIFF_WIDTH
OFF_DV = OFF_DK + DIFF_WIDTH
OFF_GQ = OFF_DV + DIFF_WIDTH
OFF_GK = OFF_GQ + GLA_WIDTH
OFF_GV = OFF_GK + GLA_WIDTH
OFF_GG = OFF_GV + GLA_WIDTH
OFF_GA = OFF_GG + GLA_WIDTH

LANE = 128
SUBLANE = 8
MXU_N = 256
ATTN_ROWS = 64
ONES_ROWS = 16
VMEM_LIMIT = 56 * 1024 * 1024

NT_DIMS = (((1,), (1,)), ((), ()))


def _dot(a, b):
    return jnp.dot(a, b, preferred_element_type=F32)


def _dot_nt(a, b):
    return lax.dot_general(a, b, NT_DIMS, preferred_element_type=F32)


def _rms(x, g):
    return x * lax.rsqrt(jnp.mean(x * x, axis=-1, keepdims=True) + EPS) * g


def _log_sigmoid(x):
    return jnp.minimum(x, 0.0) - jnp.log1p(jnp.exp(-jnp.abs(x)))


def _split_bf16(x, parts):
    out = []
    r = x
    for _ in range(parts - 1):
        p = r.astype(BF16)
        out.append(p)
        r = r - p.astype(F32)
    out.append(r.astype(BF16))
    return out


def _dot_split(x, w01, parts=3):
    return sum(_dot(p, w01) for p in _split_bf16(x, parts))


def _params(n_axes):
    return pltpu.CompilerParams(dimension_semantics=("arbitrary",) * n_axes,
                                vmem_limit_bytes=VMEM_LIMIT)


def _resident(shape):
    nd = len(shape)
    return pl.BlockSpec(shape, lambda *_: (0,) * nd, pipeline_mode=pl.Buffered(1))


def _ffn(x, g_pre, g_post, wgu_ref, wo_ref, a_scr, d_ff, chunk):
    h = _rms(x, g_pre).astype(BF16)
    for c in range(0, d_ff, chunk):
        gate = _dot(h, wgu_ref[:, c:c + chunk])
        up = _dot(h, wgu_ref[:, d_ff + c:d_ff + c + chunk])
        a_scr[:, c:c + chunk] = (gate * jax.nn.sigmoid(gate) * up).astype(BF16)
    y = _dot(a_scr[...], wo_ref[...])
    return x + 0.5 * _rms(y, g_post)


def _dense_in_kernel(*refs, d_ff, chunk, prompt):
    if prompt:
        (x_ref, g_ref, wgu_ref, wo_ref, win_ref, wkt_ref, wga_ref, wgate_ref, bg_ref,
         wdqt_ref, wvt_ref, _, _,
         x1_ref, u_ref, kn_ref, vn_ref, gq_ref, gv_ref, la_ref, gg_ref, kt_ref,
         dqt_ref, kb_ref, vt_ref, a_scr) = refs
    else:
        (x_ref, g_ref, wgu_ref, wo_ref, win_ref, wkt_ref, wga_ref, wgate_ref, bg_ref, _, _,
         x1_ref, u_ref, kn_ref, vn_ref, gq_ref, gv_ref, la_ref, gg_ref, kt_ref,
         dq_ref, a_scr) = refs
    g = g_ref[...]
    x1 = _ffn(x_ref[...], g[0:1], g[1:2], wgu_ref, wo_ref, a_scr, d_ff, chunk)
    x1_ref[...] = x1
    h = _rms(x1, g[2:3]).astype(BF16)

    ca = _dot(h, win_ref[:, 0:CONV_CH])
    cg = _dot(h, win_ref[:, CONV_CH:OFF_DQ])
    u_ref[...] = ca * jax.nn.sigmoid(cg)

    k = _dot(h, win_ref[:, OFF_DK:OFF_DV])
    v = _dot(h, win_ref[:, OFF_DV:OFF_GQ])
    tm = k.shape[0]
    for hd in range(DIFF_HEADS):
        lanes = slice(hd * DIFF_HW, (hd + 1) * DIFF_HW)
        kn_ref[pl.ds(hd, tm, stride=DIFF_HEADS), :] = k[:, lanes]
        vn_ref[pl.ds(hd, tm, stride=DIFF_HEADS), :] = v[:, lanes]
    if prompt:
        dqt_ref[...] = (_dot_nt(wdqt_ref[...], h) * (DIFF_DH ** -0.5 * LOG2E)).astype(BF16)
        kb_ref[...] = k.astype(BF16)
        vt_ref[...] = _dot_nt(wvt_ref[...], h).astype(BF16)
    else:
        dq_ref[...] = (_dot(h, win_ref[:, OFF_DQ:OFF_DK]) * (DIFF_DH ** -0.5)).astype(BF16)

    gq_ref[...] = _dot(h, win_ref[:, OFF_GQ:OFF_GK]) * (GLA_DH ** -0.5)
    gv_ref[...] = _dot(h, win_ref[:, OFF_GV:OFF_GG])
    gg_ref[...] = _dot(h, win_ref[:, OFF_GG:OFF_GA])
    ga = _dot(h, wga_ref[...])
    a_logit = _dot(ga.astype(BF16), wgate_ref[...]) + bg_ref[...]
    la_ref[...] = _log_sigmoid(a_logit) * (1.0 / GLA_TAU)

    kt_ref[...] = _dot_nt(wkt_ref[...], h)


def _dense_in(x, weights, k_all, v_all, *, layer, tm, prompt):
    m, d = x.shape
    d_ff = weights[2].shape[0]
    chunk = MXU_N
    assert m % tm == 0 and d_ff % chunk == 0
    row = lambda w: pl.BlockSpec((tm, w), lambda i: (i, 0))
    col = lambda h: pl.BlockSpec((h, tm), lambda i: (0, i))
    rows = lambda w, dt: (jax.ShapeDtypeStruct((m, w), dt), row(w))
    cols = lambda h, dt: (jax.ShapeDtypeStruct((h, m), dt), col(h))
    kv = lambda a: (jax.ShapeDtypeStruct(a.shape, a.dtype),
                    pl.BlockSpec((None, tm * DIFF_HEADS, DIFF_HW), lambda i: (layer, i, 0)))
    outs = [rows(d, F32),
            rows(CONV_CH, F32),
            kv(k_all), kv(v_all),
            rows(GLA_WIDTH, F32), rows(GLA_WIDTH, F32),
            rows(GLA_WIDTH, F32), rows(GLA_WIDTH, F32),
            cols(GLA_WIDTH, F32)]
    if prompt:
        outs += [cols(DIFF_WIDTH, BF16), rows(DIFF_WIDTH, BF16), cols(DIFF_WIDTH, BF16)]
    else:
        outs += [rows(DIFF_WIDTH, BF16)]
    return pl.pallas_call(
        functools.partial(_dense_in_kernel, d_ff=d_ff, chunk=chunk, prompt=prompt),
        grid=(m // tm,),
        in_specs=[row(d)] + [_resident(w.shape) for w in weights]
        + [pl.BlockSpec(memory_space=pl.ANY)] * 2,
        out_specs=[s for _, s in outs],
        out_shape=[s for s, _ in outs],
        input_output_aliases={1 + len(weights): 2, 2 + len(weights): 3},
        scratch_shapes=[pltpu.VMEM((tm, d_ff), BF16)],
        compiler_params=_params(1),
        name="dense_in",
    )(x, *weights, k_all, v_all)


def _dense_out_kernel(x_ref, co_ref, do_ref, go_ref, g_ref, wout_ref, wgu_ref, wo_ref,
                      x2_ref, a_scr, *, d_ff, chunk):
    g = g_ref[...]
    m = (_dot(co_ref[...], wout_ref[0:CONV_CH, :])
         + _dot(do_ref[...], wout_ref[CONV_CH:CONV_CH + DIFF_WIDTH, :])
         + _dot(go_ref[...], wout_ref[CONV_CH + DIFF_WIDTH:, :]))
    x = x_ref[...] + _rms(m, g[3:4])
    x2_ref[...] = _ffn(x, g[4:5], g[5:6], wgu_ref, wo_ref, a_scr, d_ff, chunk)


def _dense_out(x, co, do, go, g, wout, wgu, wo, *, tm):
    m, d = x.shape
    d_ff = wo.shape[0]
    chunk = MXU_N
    assert m % tm == 0 and d_ff % chunk == 0
    row = lambda w: pl.BlockSpec((tm, w), lambda i: (i, 0))
    weights = (g, wout, wgu, wo)
    return pl.pallas_call(
        functools.partial(_dense_out_kernel, d_ff=d_ff, chunk=chunk),
        grid=(m // tm,),
        in_specs=[row(d), row(CONV_CH), row(DIFF_WIDTH), row(GLA_WIDTH)]
        + [_resident(w.shape) for w in weights],
        out_specs=row(d),
        out_shape=jax.ShapeDtypeStruct((m, d), F32),
        scratch_shapes=[pltpu.VMEM((tm, d_ff), BF16)],
        compiler_params=_params(1),
        name="dense_out",
    )(x, co, do, go, *weights)


def _conv_kernel(u_ref, buf_ref, w_ref, b_ref, lg_ref, lb_ref, pw_ref, o_ref, nc_ref, scr,
                 *, nseq, t, tt):
    hist = CONV_WIDTH - 1
    lead = CONV_HALO - hist
    w = w_ref[...]
    bias = b_ref[...]
    lg = lg_ref[...]
    lb = lb_ref[...]
    pw = pw_ref[...]
    scr[0:lead, :] = jnp.zeros((lead, CONV_CH), F32)

    for s in range(nseq):
        scr[lead:CONV_HALO, :] = buf_ref[s]
        scr[CONV_HALO:CONV_HALO + t, :] = u_ref[s]
        nc_ref[s] = scr[t + lead:t + CONV_HALO, :]

        def tile(i, carry):
            t0 = pl.multiple_of(i * tt, tt)
            win = scr[pl.ds(t0, tt + CONV_HALO), :]
            acc = jnp.zeros((tt, CONV_CH), F32) + bias
            for r in range(SUBLANE):
                taps = [j for j in range(CONV_WIDTH) if (lead + j) % SUBLANE == r]
                sh = win if r == 0 else win[r:r + tt + CONV_HALO - SUBLANE, :]
                for j in taps:
                    a = lead + j - r
                    acc = acc + w[j:j + 1, :] * sh[a:a + tt, :]
            mu = jnp.mean(acc, axis=-1, keepdims=True)
            xc = acc - mu
            y = xc * lax.rsqrt(jnp.mean(xc * xc, axis=-1, keepdims=True) + EPS) * lg + lb
            y = y * jax.nn.sigmoid(y)
            o_ref[s, pl.ds(t0, tt), :] = _dot(y.astype(BF16), pw).astype(BF16)
            return carry

        lax.fori_loop(0, t // tt, tile, 0)


def _conv(u, buf, w, b, lg, lb, pw, *, nb, t):
    tt = min(t, 256)
    nseq = 1 if t >= 256 else min(nb, 16)
    assert t % tt == 0 and tt % SUBLANE == 0 and nb % nseq == 0
    hist = CONV_WIDTH - 1
    u3 = u.reshape(nb, t, CONV_CH)
    small = lambda a: _resident(a.shape)
    out, new_conv = pl.pallas_call(
        functools.partial(_conv_kernel, nseq=nseq, t=t, tt=tt),
        grid=(nb // nseq,),
        in_specs=[pl.BlockSpec((nseq, t, CONV_CH), lambda i: (i, 0, 0)),
                  pl.BlockSpec((nseq, hist, CONV_CH), lambda i: (i, 0, 0)),
                  small(w), small(b), small(lg), small(lb), small(pw)],
        out_specs=[pl.BlockSpec((nseq, t, CONV_CH), lambda i: (i, 0, 0)),
                   pl.BlockSpec((nseq, hist, CONV_CH), lambda i: (i, 0, 0))],
        out_shape=[jax.ShapeDtypeStruct((nb, t, CONV_CH), BF16),
                   jax.ShapeDtypeStruct((nb, hist, CONV_CH), F32)],
        scratch_shapes=[pltpu.VMEM((CONV_HALO + t, CONV_CH), F32)],
        compiler_params=_params(1),
        name="conv",
    )(u3, buf, w, b, lg, lb, pw)
    return out.reshape(nb * t, CONV_CH), new_conv


def _fold_rows(x, op):
    n = x.shape[0]
    while n > SUBLANE:
        n //= 2
        x = op(x[:n], x[n:2 * n])
    return x


def _fold_list(xs, op):
    while len(xs) > 1:
        xs = [op(xs[i], xs[i + 1]) for i in range(0, len(xs) - 1, 2)] + (xs[-1:] if len(xs) % 2 else [])
    return xs[0]


def _lam(lam_ref, lam_init):
    dl = lam_ref[...]
    s1 = jnp.sum(dl[0:1] * dl[1:2], axis=-1, keepdims=True)
    s2 = jnp.sum(dl[2:3] * dl[3:4], axis=-1, keepdims=True)
    return jnp.exp(s1) - jnp.exp(s2) + lam_init


def _attn_p_kernel(qt_ref, k_ref, vt_ref, lam_ref, sg_ref, o_ref, m_scr, acc_scr,
                   qs_scr, sa_scr, sb_scr, *, tq, tk, lam_init):
    qi = pl.program_id(2)
    qt = qt_ref[...].astype(F32)
    sub = lax.broadcasted_iota(jnp.int32, (DIFF_HW, tq), 0)
    qs_scr[...] = jnp.concatenate([jnp.where(sub < DIFF_DH, qt, 0.0),
                                   jnp.where(sub >= DIFF_DH, qt, 0.0)], axis=1).astype(BF16)
    m_scr[...] = jnp.full(m_scr.shape, -jnp.inf, F32)
    acc_scr[...] = jnp.zeros(acc_scr.shape, F32)

    tiles = [slice(c * MXU_N, (c + 1) * MXU_N) for c in range(2 * tq // MXU_N)]

    def live_tiles(diag):
        if diag is None:
            return tiles
        return [cs for cs in tiles if cs.start % tq + MXU_N - 1 >= diag * tk]

    def scores(j, s_ref, diag=None, only=None):
        k0 = pl.multiple_of(j * tk, tk)
        kb = k_ref[pl.ds(k0, tk), :]
        for cs in (only or live_tiles(diag)):
            s_ref[:, cs] = _dot(kb, qs_scr[:, cs])

    def softmax_pv(j, s_ref, diag=None, only=None):
        k0 = pl.multiple_of(j * tk, tk)
        vtb = jnp.concatenate([vt_ref[:, pl.ds(k0, tk)], jnp.ones((ONES_ROWS, tk), BF16)], axis=0)
        chunks = range(0, tk, ATTN_ROWS)
        for cs in (only or live_tiles(diag)):
            q0 = cs.start % tq
            masked = diag is not None and q0 < (diag + 1) * tk - 1

            def chunk(r0):
                s = s_ref[r0:r0 + ATTN_ROWS, cs]
                if masked:
                    key = lax.broadcasted_iota(jnp.int32, (ATTN_ROWS, MXU_N), 0) + (diag * tk + r0)
                    qry = lax.broadcasted_iota(jnp.int32, (ATTN_ROWS, MXU_N), 1) + q0
                    s = jnp.where(key <= qry, s, -jnp.inf)
                return s

            m_blk = _fold_list([_fold_rows(chunk(r0), jnp.maximum) for r0 in chunks], jnp.maximum)
            m_prev = m_scr[:, cs]
            m_new = jnp.maximum(m_prev, jnp.max(m_blk, axis=0, keepdims=True))
            alpha = jnp.exp2(m_prev - m_new)
            p = jnp.concatenate([jnp.exp2(chunk(r0) - m_new).astype(BF16) for r0 in chunks], axis=0)
            acc_scr[:, cs] = alpha * acc_scr[:, cs] + _dot(vtb, p)
            m_scr[:, cs] = m_new

    scores(0, sa_scr)

    def body(i, carry):
        j = 2 * i
        for cs in tiles:
            scores(j + 1, sb_scr, only=[cs])
            softmax_pv(j, sa_scr, only=[cs])
        for cs in tiles:
            scores(j + 2, sa_scr, only=[cs])
            softmax_pv(j + 1, sb_scr, only=[cs])
        return carry

    lax.fori_loop(0, qi, body, 0)
    first = live_tiles(0)
    later = live_tiles(1)
    for n, cs in enumerate(first):
        if n < len(later):
            scores(2 * qi + 1, sb_scr, diag=1, only=[later[n]])
        softmax_pv(2 * qi, sa_scr, diag=0, only=[cs])
    softmax_pv(2 * qi + 1, sb_scr, diag=1)

    o = acc_scr[0:DIFF_HW, :] / acc_scr[DIFF_HW:DIFF_HW + 1, :]
    lam = _lam(lam_ref, lam_init)
    o = (o[:, 0:tq] - lam * o[:, tq:2 * tq]).T
    o_ref[...] = (_rms(o, sg_ref[...]) * (1.0 - lam_init)).astype(BF16)


def _attn_prompt(dqt, kb, vt, lam_p, sg, *, nb, t, lam_init):
    tq = 1024
    tk = tq // 2
    assert t % tq == 0 and tk % MXU_N == 0 and tk % ATTN_ROWS == 0
    nq = t // tq
    return pl.pallas_call(
        functools.partial(_attn_p_kernel, tq=tq, tk=tk, lam_init=lam_init),
        grid=(nb, DIFF_HEADS, nq),
        in_specs=[pl.BlockSpec((DIFF_HW, tq), lambda b, h, i: (h, b * nq + i)),
                  pl.BlockSpec((t, DIFF_HW), lambda b, h, i: (b, h)),
                  pl.BlockSpec((DIFF_HW, t), lambda b, h, i: (h, b)),
                  _resident(lam_p.shape), _resident(sg.shape)],
        out_specs=pl.BlockSpec((tq, DIFF_HW), lambda b, h, i: (b * nq + i, h)),
        out_shape=jax.ShapeDtypeStruct((nb * t, DIFF_WIDTH), BF16),
        scratch_shapes=[pltpu.VMEM((1, 2 * tq), F32),
                        pltpu.VMEM((DIFF_HW + ONES_ROWS, 2 * tq), F32),
                        pltpu.VMEM((DIFF_HW, 2 * tq), BF16),
                        pltpu.VMEM((tk, 2 * tq), F32),
                        pltpu.VMEM((tk, 2 * tq), F32)],
        compiler_params=_params(3),
        name="attn_prompt",
    )(dqt, kb, vt, lam_p, sg)


def _attn_s_kernel(pt_ref, q_ref, kn_ref, vn_ref, lam_ref, sg_ref, ck_hbm, cv_hbm, o_ref,
                   kbuf, vbuf, bias, sem, *, layer, n_pages, prow, ts, lam_init):
    b = pl.program_id(0)
    nb = pl.num_programs(0)
    slot = lax.rem(b, 2)
    past = n_pages * prow
    new = ts * DIFF_HEADS
    rows = kbuf.shape[1]
    nq = 2 * DIFF_HEADS * ts

    def page_copies(bb, sl):
        cps = []
        for p in range(n_pages):
            pg = pt_ref[bb, p]
            dst = pl.ds(p * prow, prow)
            cps.append(pltpu.make_async_copy(ck_hbm.at[layer, pg], kbuf.at[sl, dst], sem.at[0, sl]))
            cps.append(pltpu.make_async_copy(cv_hbm.at[layer, pg], vbuf.at[sl, dst], sem.at[1, sl]))
        return cps

    def start_all(cps):
        for i, cp in enumerate(cps):
            cp.start(priority=i % 2)

    @pl.when(b == 0)
    def _():
        for sl in range(2):
            kbuf[sl, past:rows, :] = jnp.zeros((rows - past, DIFF_HW), F32)
            vbuf[sl, past:rows, :] = jnp.zeros((rows - past, DIFF_HW), F32)
        r = lax.broadcasted_iota(jnp.int32, (nq, rows), 0)
        c = lax.broadcasted_iota(jnp.int32, (nq, rows), 1)
        same_head = lax.rem(c, DIFF_HEADS) == r // (2 * ts)
        causal = c // DIFF_HEADS - past // DIFF_HEADS <= lax.rem(r, ts)
        bias[...] = jnp.where(jnp.logical_and(same_head, causal), 0.0, -jnp.inf)
        start_all(page_copies(0, 0))

    @pl.when(b + 1 < nb)
    def _():
        start_all(page_copies(b + 1, 1 - slot))

    for cp in page_copies(b, slot):
        cp.wait()

    kbuf[slot, past:past + new, :] = kn_ref[...]
    vbuf[slot, past:past + new, :] = vn_ref[...]

    q = q_ref[...].astype(F32)
    half = lax.broadcasted_iota(jnp.int32, (ts, DIFF_HW), 1) // DIFF_DH
    qrows = []
    for h in range(DIFF_HEADS):
        qh = q[:, h * DIFF_HW:(h + 1) * DIFF_HW]
        qrows += [jnp.where(half == 0, qh, 0.0), jnp.where(half == 1, qh, 0.0)]
    qrows = jnp.concatenate(qrows, axis=0).astype(BF16)
    s = _dot_nt(qrows, kbuf[slot].astype(BF16)) + bias[...]
    m = jnp.max(s, axis=-1, keepdims=True)
    p = jnp.exp(s - m)
    l = jnp.sum(p, axis=-1, keepdims=True)
    pv = _dot(p.astype(BF16), vbuf[slot].astype(BF16)) / l
    lam = _lam(lam_ref, lam_init)
    sg = sg_ref[...]
    outs = []
    for h in range(DIFF_HEADS):
        o = pv[2 * h * ts:(2 * h + 1) * ts] - lam * pv[(2 * h + 1) * ts:(2 * h + 2) * ts]
        outs.append(_rms(o, sg) * (1.0 - lam_init))
    o_ref[...] = jnp.concatenate(outs, axis=-1).astype(BF16)


def _attn_sample(page_table, dq, kn, vn, lam_p, sg, ck, cv, *, layer, nb, ts, lam_init):
    n_pages = page_table.shape[1]
    prow = ck.shape[2]
    new = ts * DIFF_HEADS
    rows = n_pages * prow + LANE
    nq = 2 * DIFF_HEADS * ts
    assert new <= LANE and ts % SUBLANE == 0
    const = lambda a: pl.BlockSpec(a.shape, lambda b, pt: (0,) * a.ndim, pipeline_mode=pl.Buffered(1))
    grid_spec = pltpu.PrefetchScalarGridSpec(
        num_scalar_prefetch=1,
        grid=(nb,),
        in_specs=[pl.BlockSpec((ts, DIFF_WIDTH), lambda b, pt: (b, 0)),
                  pl.BlockSpec((None, new, DIFF_HW), lambda b, pt: (layer, b, 0)),
                  pl.BlockSpec((None, new, DIFF_HW), lambda b, pt: (layer, b, 0)),
                  const(lam_p), const(sg),
                  pl.BlockSpec(memory_space=pl.ANY), pl.BlockSpec(memory_space=pl.ANY)],
        out_specs=pl.BlockSpec((ts, DIFF_WIDTH), lambda b, pt: (b, 0)),
        scratch_shapes=[pltpu.VMEM((2, rows, DIFF_HW), F32),
                        pltpu.VMEM((2, rows, DIFF_HW), F32),
                        pltpu.VMEM((nq, rows), F32),
                        pltpu.SemaphoreType.DMA((2, 2))],
    )
    return pl.pallas_call(
        functools.partial(_attn_s_kernel, layer=layer, n_pages=n_pages, prow=prow, ts=ts,
                          lam_init=lam_init),
        grid_spec=grid_spec,
        out_shape=jax.ShapeDtypeStruct((nb * ts, DIFF_WIDTH), BF16),
        compiler_params=_params(1),
        name="attn_sample",
    )(page_table, dq, kn, vn, lam_p, sg, ck, cv)


def _gla_kernel(q_ref, v_ref, la_ref, gg_ref, kt_ref, s0_ref, gn_ref,
                o_ref, sout_ref, s_scr, *, c, groups):
    step = pl.program_id(1)
    r = c // groups

    hr = lax.broadcasted_iota(jnp.int32, (GLA_WIDTH, GLA_WIDTH), 0) // GLA_DH
    hc = lax.broadcasted_iota(jnp.int32, (GLA_WIDTH, GLA_WIDTH), 1) // GLA_DH
    same_head = hr == hc

    @pl.when(step == 0)
    def _():
        er = lax.broadcasted_iota(jnp.int32, (GLA_DH, GLA_WIDTH), 0)
        ec = lax.broadcasted_iota(jnp.int32, (GLA_DH, GLA_WIDTH), 1)
        expand = jnp.where(lax.rem(ec, GLA_DH) == er, 1.0, 0.0).astype(BF16)
        for gi in range(groups):
            s_scr[gi] = jnp.where(same_head, _dot_split(s0_ref[gi], expand), 0.0)

    row = lax.broadcasted_iota(jnp.int32, (c, c), 0)
    col = lax.broadcasted_iota(jnp.int32, (c, c), 1)
    if groups > 1:
        causal = jnp.logical_and(col <= row, (row // r) == (col // r))
    else:
        causal = col <= row
    ltri = jnp.where(causal, 1.0, 0.0).astype(BF16)

    cum = sum(_dot(ltri, part) for part in _split_bf16(la_ref[...], 3))
    cum_t = cum.T

    q = q_ref[...]
    kt = kt_ref[...]
    v = v_ref[...]
    vb = v.astype(BF16)
    qd = q * jnp.exp(cum)
    if groups == 1:
        qt = q * jnp.exp(cum - cum[c // 2:c // 2 + 1, :])
        ktt = kt * jnp.exp(cum_t[:, c // 2:c // 2 + 1] - cum_t)
    else:
        qt = qd
        ktt = kt * jnp.exp(-cum_t)
    kttb = ktt.astype(BF16)

    head_l = lax.broadcasted_iota(jnp.int32, (c, GLA_WIDTH), 1) // GLA_DH
    heads = range(GLA_HEADS)
    q4 = jnp.concatenate([jnp.where(head_l == h, qt, 0.0) for h in heads], axis=0).astype(BF16)
    att4 = _dot(q4, kttb)
    att = jnp.concatenate([jnp.where(causal, att4[h * c:(h + 1) * c], 0.0) for h in heads],
                          axis=1).astype(BF16)
    v4 = jnp.concatenate([jnp.where(head_l == h, v, 0.0) for h in heads], axis=0).astype(BF16)
    o = _dot(att, v4)

    colg = lax.broadcasted_iota(jnp.int32, (GLA_WIDTH, c), 1) // r
    inter = []
    for gi in range(groups):
        s_old = s_scr[gi]
        inter.append(_dot(qd[gi * r:(gi + 1) * r, :].astype(BF16), s_old.astype(BF16)))
        last_t = cum_t[:, (gi + 1) * r - 1:(gi + 1) * r]
        if groups > 1:
            ing = colg == gi
            kd_t = jnp.where(ing, kt * jnp.exp(jnp.where(ing, last_t - cum_t, 0.0)), 0.0)
        else:
            kd_t = kt * jnp.exp(last_t - cum_t)
        upd = _dot(kd_t.astype(BF16), vb)
        s_scr[gi] = s_old * jnp.exp(last_t) + jnp.where(same_head, upd, 0.0)
    o = o + (inter[0] if groups == 1 else jnp.concatenate(inter, axis=0))

    avg = jnp.where(same_head, 1.0 / GLA_DH, 0.0).astype(BF16)
    msq = _dot_split(o * o, avg, parts=2)
    gg = gg_ref[...]
    o_ref[...] = (o * lax.rsqrt(msq + EPS) * gn_ref[...] * (gg * jax.nn.sigmoid(gg))).astype(BF16)

    @pl.when(step == pl.num_programs(1) - 1)
    def _():
        cr = lax.broadcasted_iota(jnp.int32, (GLA_WIDTH, GLA_DH), 0)
        cc = lax.broadcasted_iota(jnp.int32, (GLA_WIDTH, GLA_DH), 1)
        compact = jnp.where(lax.rem(cr, GLA_DH) == cc, 1.0, 0.0).astype(BF16)
        for gi in range(groups):
            sout_ref[gi] = _dot_split(s_scr[gi], compact)


def _gla(gq, gv, la, gg, kt, s0, gn, *, nseq, t):
    c = LANE
    if t >= c:
        assert t % c == 0
        groups, nb, nsteps = 1, nseq, t // c
    else:
        assert c % t == 0 and nseq % (c // t) == 0
        groups, nb, nsteps = c // t, nseq // (c // t), 1
    m = nseq * t
    row = pl.BlockSpec((c, GLA_WIDTH), lambda b, s: (b * nsteps + s, 0))
    colb = pl.BlockSpec((GLA_WIDTH, c), lambda b, s: (0, b * nsteps + s))
    st = pl.BlockSpec((groups, GLA_WIDTH, GLA_DH), lambda b, s: (b, 0, 0))
    return pl.pallas_call(
        functools.partial(_gla_kernel, c=c, groups=groups),
        grid=(nb, nsteps),
        in_specs=[row, row, row, row, colb, st,
                  pl.BlockSpec(gn.shape, lambda b, s: (0, 0))],
        out_specs=[row, st],
        out_shape=[jax.ShapeDtypeStruct((m, GLA_WIDTH), BF16),
                   jax.ShapeDtypeStruct((nseq, GLA_WIDTH, GLA_DH), F32)],
        scratch_shapes=[pltpu.VMEM((groups, GLA_WIDTH, GLA_WIDTH), F32)],
        compiler_params=_params(2),
        name="gla",
    )(gq, gv, la, gg, kt, s0, gn)


def _layer_weights(l, norm_gain, ffn_w_in, ffn_w_out, w_in, w_out, conv_w, conv_b, conv_ln_g,
                   conv_ln_b, conv_pw, diff_lam, diff_subln_g, gla_w_gate, gla_b_gate, gla_norm_g):
    pad_r = LANE - GLA_RANK
    wga = jnp.pad(w_in[l][:, OFF_GA:], ((0, 0), (0, pad_r))).astype(BF16)
    wgate = jnp.pad(gla_w_gate[l], ((0, pad_r), (0, 0))).astype(BF16)
    win = w_in[l][:, :OFF_GA].astype(BF16)
    dense_in = [norm_gain[l], ffn_w_in[l, 0].astype(BF16), ffn_w_out[l, 0].astype(BF16), win,
                win[:, OFF_GK:OFF_GV].T, wga, wgate, gla_b_gate[l].reshape(1, GLA_WIDTH)]
    return dict(
        dense_in_s=dense_in,
        dense_in_p=dense_in + [win[:, OFF_DQ:OFF_DK].T, win[:, OFF_DV:OFF_GQ].T],
        dense_out=[norm_gain[l], w_out[l].astype(BF16), ffn_w_in[l, 1].astype(BF16),
                   ffn_w_out[l, 1].astype(BF16)],
        conv=[conv_w[l], conv_b[l].reshape(1, CONV_CH), conv_ln_g[l].reshape(1, CONV_CH),
              conv_ln_b[l].reshape(1, CONV_CH), conv_pw[l].astype(BF16)],
        lam=diff_lam[l], sg=diff_subln_g[l].reshape(1, DIFF_HW),
        gn=jnp.tile(gla_norm_g[l], GLA_HEADS).reshape(1, GLA_WIDTH),
    )


def kernel(x_prompt, x_sample, cache_k, cache_v, page_table, state_conv, state_gla, norm_gain,
           ffn_w_in, ffn_w_out, w_in, w_out, conv_w, conv_b, conv_ln_g, conv_ln_b, conv_pw,
           diff_lam, diff_subln_g, gla_w_gate, gla_b_gate, gla_norm_g):
    bp, tp, d = x_prompt.shape
    bs, ts, _ = x_sample.shape
    depth = norm_gain.shape[0]
    n_pool, page = cache_k.shape[1], cache_k.shape[2]
    ck = cache_k.reshape(depth, n_pool, page * DIFF_HEADS, DIFF_HW)
    cv = cache_v.reshape(depth, n_pool, page * DIFF_HEADS, DIFF_HW)

    xp = x_prompt.reshape(bp * tp, d)
    xs = x_sample.reshape(bs * ts, d)
    conv0 = jnp.zeros((bp, CONV_WIDTH - 1, CONV_CH), F32)
    s0p = jnp.zeros((bp, GLA_WIDTH, GLA_DH), F32)
    kp = jnp.zeros((depth, bp * tp * DIFF_HEADS, DIFF_HW), F32)
    vp = jnp.zeros((depth, bp * tp * DIFF_HEADS, DIFF_HW), F32)
    ksm = jnp.zeros((depth, bs * ts * DIFF_HEADS, DIFF_HW), F32)
    vsm = jnp.zeros((depth, bs * ts * DIFF_HEADS, DIFF_HW), F32)
    outs = [[] for _ in range(4)]
    for l in range(depth):
        w = _layer_weights(l, norm_gain, ffn_w_in, ffn_w_out, w_in, w_out, conv_w, conv_b,
                           conv_ln_g, conv_ln_b, conv_pw, diff_lam, diff_subln_g, gla_w_gate,
                           gla_b_gate, gla_norm_g)
        lam_init = 0.8 - 0.6 * math.exp(-0.3 * l)

        x1, u, kp, vp, gq, gv, la, gg, kt, dqt, kb, vt = _dense_in(
            xp, w["dense_in_p"], kp, vp, layer=l, tm=512, prompt=True)
        co, ncp = _conv(u, conv0, *w["conv"], nb=bp, t=tp)
        do = _attn_prompt(dqt, kb, vt, w["lam"], w["sg"], nb=bp, t=tp, lam_init=lam_init)
        go, sp = _gla(gq, gv, la, gg, kt, s0p, w["gn"], nseq=bp, t=tp)
        xp = _dense_out(x1, co, do, go, *w["dense_out"], tm=512)
        outs[0].append(ncp)
        outs[1].append(sp.reshape(bp, GLA_HEADS, GLA_DH, GLA_DH))

        x1, u, ksm, vsm, gq, gv, la, gg, kt, dq = _dense_in(
            xs, w["dense_in_s"], ksm, vsm, layer=l, tm=256, prompt=False)
        co, ncs = _conv(u, state_conv[l], *w["conv"], nb=bs, t=ts)
        do = _attn_sample(page_table, dq, ksm, vsm, w["lam"], w["sg"], ck, cv, layer=l, nb=bs,
                          ts=ts, lam_init=lam_init)
        go, ss = _gla(gq, gv, la, gg, kt, state_gla[l].reshape(bs, GLA_WIDTH, GLA_DH),
                      w["gn"], nseq=bs, t=ts)
        xs = _dense_out(x1, co, do, go, *w["dense_out"], tm=512)
        outs[2].append(ncs)
        outs[3].append(ss.reshape(bs, GLA_HEADS, GLA_DH, GLA_DH))

    kv_shape_p = (depth, bp, tp, DIFF_HEADS, DIFF_HW)
    kv_shape_s = (depth, bs, ts, DIFF_HEADS, DIFF_HW)
    ncp, sp, ncs, ss = (jnp.stack(o) for o in outs)
    return (xp.reshape(bp, tp, d), xs.reshape(bs, ts, d), kp.reshape(kv_shape_p),
            vp.reshape(kv_shape_p), ncp, sp, ksm.reshape(kv_shape_s), vsm.reshape(kv_shape_s), ncs, ss)
```
